```python
import jax, jax.numpy as jnp
from jax import lax
import numpy as np

D_MODEL = 1024
BATCH = 32
SEQ = 2048
DEPTH = 1

N_HEADS = 8
HEAD_DIM = 64
ATTN_WIDTH = N_HEADS * HEAD_DIM
IDX_HEADS = 8
IDX_DIM = 64
TOPK_MAX = 256
Q_BLOCK = 64
POOL_WIDTH = D_MODEL - ATTN_WIDTH
POOL_WINDOWS = (2, 4, 8, 16)
N_POOL_GROUPS = len(POOL_WINDOWS)
POOL_GROUP_DIM = POOL_WIDTH // N_POOL_GROUPS
D_FF = -(-8 * D_MODEL // (3 * 256)) * 256
ROPE_THETA = 10000.0
LN_EPS = 1e-5
ALPHA = (2.0 * DEPTH) ** 0.25
BETA = (8.0 * DEPTH) ** -0.25
N_MOD = 6
IN_SPLITS = (ATTN_WIDTH, ATTN_WIDTH, ATTN_WIDTH, POOL_WIDTH,
             IDX_HEADS * IDX_DIM, IDX_DIM, IDX_HEADS)
W_IN_COLS = sum(IN_SPLITS)

kernel_name = "hymba_dsa_pool_deepnorm_adaln"


def layer_norm(x, g, b):
    xf = x.astype(jnp.float32)
    mu = jnp.mean(xf, axis=-1, keepdims=True)
    var = jnp.mean(jnp.square(xf - mu), axis=-1, keepdims=True)
    y = (xf - mu) * lax.rsqrt(var + LN_EPS)
    return (y * g.astype(jnp.float32) + b.astype(jnp.float32)).astype(x.dtype)


def rope(x, pos):
    d = x.shape[-1]
    inv_freq = ROPE_THETA ** (-jnp.arange(0, d, 2, dtype=jnp.float32) / d)
    ang = pos[:, None] * inv_freq[None, :]
    cos = jnp.cos(ang)[None, :, None, :]
    sin = jnp.sin(ang)[None, :, None, :]
    xf = x.astype(jnp.float32)
    x1, x2 = jnp.split(xf, 2, axis=-1)
    out = jnp.concatenate([x1 * cos - x2 * sin, x2 * cos + x1 * sin], axis=-1)
    return out.astype(x.dtype)


def dsa_sparse_attention(q, k, v, iq, ik, iw):
    B, S = q.shape[0], q.shape[1]
    k_sel = min(TOPK_MAX, S // 4)
    nblk = S // Q_BLOCK
    pos_k = jnp.arange(S)
    ik_f = ik.astype(jnp.float32)

    def to_blocks(a):
        return a.reshape((B, nblk, Q_BLOCK) + a.shape[2:]).swapaxes(0, 1)

    def one_block(args):
        qb, iqb, iwb, t0 = args
        pos_q = t0 + jnp.arange(Q_BLOCK)
        causal = pos_k[None, :] <= pos_q[:, None]
        logits = jnp.einsum('bqhd,bsd->bqhs', iqb.astype(jnp.float32), ik_f) * (IDX_DIM ** -0.5)
        w = iwb.astype(jnp.float32) * (IDX_HEADS ** -0.5)
        score = jnp.einsum('bqh,bqhs->bqs', w, jax.nn.relu(logits))
        score = jnp.where(causal[None], score, -jnp.inf)
        _, idx = lax.top_k(score, k_sel)
        kg = jax.vmap(lambda kb, ib: kb[ib])(k, idx)
        vg = jax.vmap(lambda vb, ib: vb[ib])(v, idx)
        valid = idx <= pos_q[None, :, None]
        s = jnp.einsum('bqhd,bqkhd->bhqk', qb.astype(jnp.float32),
                       kg.astype(jnp.float32)) * (HEAD_DIM ** -0.5)
        s = jnp.where(valid[:, None], s, -jnp.inf)
        p = jax.nn.softmax(s, axis=-1)
        o = jnp.einsum('bhqk,bqkhd->bqhd', p, vg.astype(jnp.float32))
        return o.astype(q.dtype)

    starts = jnp.arange(nblk, dtype=jnp.int32) * Q_BLOCK
    out = lax.map(one_block, (to_blocks(q), to_blocks(iq), to_blocks(iw), starts))
    return out.swapaxes(0, 1).reshape(B, S, N_HEADS * HEAD_DIM)


def causal_multiscale_pool(u):
    B, S = u.shape[0], u.shape[1]
    ug = u.reshape(B, S, N_POOL_GROUPS, POOL_GROUP_DIM).astype(jnp.float32)
    cs = jnp.concatenate([jnp.zeros((B, 1, N_POOL_GROUPS, POOL_GROUP_DIM), jnp.float32),
                          jnp.cumsum(ug, axis=1)], axis=1)
    t = jnp.arange(S)
    outs = []
    for g, win in enumerate(POOL_WINDOWS):
        start = jnp.maximum(t + 1 - win, 0)
        sums = cs[:, t + 1, g] - cs[:, start, g]
        cnt = (t + 1 - start).astype(jnp.float32)
        outs.append(sums / cnt[None, :, None])
    pooled = jnp.stack(outs, axis=2)
    return (pooled - ug).astype(u.dtype)


def hybrid_mixer(h, w_in, w_pool, pool_scale, w_o):
    B, S = h.shape[0], h.shape[1]
    proj = h @ w_in
    offs = list(np.cumsum(IN_SPLITS)[:-1])
    q, k, v, u, iq, ik, iw = jnp.split(proj, offs, axis=-1)
    pos = jnp.arange(S, dtype=jnp.float32)
    q = rope(q.reshape(B, S, N_HEADS, HEAD_DIM), pos)
    k = rope(k.reshape(B, S, N_HEADS, HEAD_DIM), pos)
    v = v.reshape(B, S, N_HEADS, HEAD_DIM)
    iq = rope(iq.reshape(B, S, IDX_HEADS, IDX_DIM), pos)
    ik = rope(ik[:, :, None, :], pos)[:, :, 0, :]
    attn = dsa_sparse_attention(q, k, v, iq, ik, iw)
    pooled = causal_multiscale_pool(u)
    pool_out = jnp.einsum('bsgc,gcd->bsgd', pooled, w_pool).reshape(B, S, POOL_WIDTH) * pool_scale
    return jnp.concatenate([attn, pool_out], axis=-1) @ w_o


def swiglu(h, w_gate, w_up, w_down):
    return (jax.nn.silu(h @ w_gate) * (h @ w_up)) @ w_down


def setup_inputs(seed: int = 0) -> dict:
    key = jax.random.key(seed)
    ks = jax.random.split(key, 16)
    f32 = jnp.float32
    x = jax.random.normal(ks[0], (BATCH, SEQ, D_MODEL), f32)
    c = jax.random.normal(ks[1], (BATCH, D_MODEL), f32)
    w_mod = jax.random.normal(ks[2], (DEPTH, D_MODEL, N_MOD * D_MODEL), f32) * (0.5 * D_MODEL ** -0.5)
    b_mod = 0.02 * jax.random.normal(ks[3], (DEPTH, N_MOD * D_MODEL), f32)
    col_scale = np.concatenate([
        np.full(ATTN_WIDTH, 1.0), np.full(ATTN_WIDTH, 1.0),
        np.full(ATTN_WIDTH, BETA), np.full(POOL_WIDTH, BETA),
        np.full(IDX_HEADS * IDX_DIM, 1.0), np.full(IDX_DIM, 1.0), np.full(IDX_HEADS, 1.0)
    ]).astype(np.float32) * (D_MODEL ** -0.5)
    w_in = jax.random.normal(ks[4], (DEPTH, D_MODEL, W_IN_COLS), f32) * jnp.asarray(col_scale)
    w_pool = jax.random.normal(ks[5], (DEPTH, N_POOL_GROUPS, POOL_GROUP_DIM, POOL_GROUP_DIM), f32) * (POOL_GROUP_DIM ** -0.5)
    pool_scale = 1.0 + 0.1 * jax.random.normal(ks[6], (DEPTH, POOL_WIDTH), f32)
    w_o = jax.random.normal(ks[7], (DEPTH, D_MODEL, D_MODEL), f32) * (BETA * D_MODEL ** -0.5)
    ln1_g = 1.0 + 0.02 * jax.random.normal(ks[8], (DEPTH, D_MODEL), f32)
    ln1_b = 0.02 * jax.random.normal(ks[9], (DEPTH, D_MODEL), f32)
    w_gate = jax.random.normal(ks[10], (DEPTH, D_MODEL, D_FF), f32) * (D_MODEL ** -0.5)
    w_up = jax.random.normal(ks[11], (DEPTH, D_MODEL, D_FF), f32) * (BETA * D_MODEL ** -0.5)
    w_down = jax.random.normal(ks[12], (DEPTH, D_FF, D_MODEL), f32) * (BETA * D_FF ** -0.5)
    ln2_g = 1.0 + 0.02 * jax.random.normal(ks[13], (DEPTH, D_MODEL), f32)
    ln2_b = 0.02 * jax.random.normal(ks[14], (DEPTH, D_MODEL), f32)
    return {"x": x, "c": c, "w_mod": w_mod, "b_mod": b_mod, "w_in": w_in,
            "w_pool": w_pool, "pool_scale": pool_scale, "w_o": w_o,
            "ln1_g": ln1_g, "ln1_b": ln1_b, "w_gate": w_gate, "w_up": w_up,
            "w_down": w_down, "ln2_g": ln2_g, "ln2_b": ln2_b}


def reference(x, c, w_mod, b_mod, w_in, w_pool, pool_scale, w_o,
              ln1_g, ln1_b, w_gate, w_up, w_down, ln2_g, ln2_b):
    for l in range(DEPTH):
        mod = jax.nn.silu(c) @ w_mod[l] + b_mod[l]
        shift1, scale1, gate1, shift2, scale2, gate2 = [m[:, None, :] for m in jnp.split(mod, N_MOD, axis=-1)]
        h = x * (1.0 + scale1) + shift1
        mix = hybrid_mixer(h, w_in[l], w_pool[l], pool_scale[l], w_o[l])
        x = layer_norm(ALPHA * x + gate1 * mix, ln1_g[l], ln1_b[l])
        h = x * (1.0 + scale2) + shift2
        ffn = swiglu(h, w_gate[l], w_up[l], w_down[l])
        x = layer_norm(ALPHA * x + gate2 * ffn, ln2_g[l], ln2_b[l])
    return x
```

```python
import functools

import numpy as np
import jax
import jax.numpy as jnp
from jax import lax
from jax.experimental import pallas as pl
from jax.experimental.pallas import tpu as pltpu

N_HEADS = 8
HEAD_DIM = 64
ATTN_WIDTH = N_HEADS * HEAD_DIM
IDX_HEADS = 8
IDX_DIM = 64
TOPK_MAX = 256
POOL_WINDOWS = (2, 4, 8, 16)
N_POOL_GROUPS = len(POOL_WINDOWS)
POOL_GROUP_DIM = 128
POOL_WIDTH = N_POOL_GROUPS * POOL_GROUP_DIM
ROPE_THETA = 10000.0
LN_EPS = 1e-5
N_MOD = 6

LANES = 128
POOL_HALO = 16
VMEM_LIMIT_BYTES = 56 * 1024 * 1024

F32 = jnp.float32
BF16 = jnp.bfloat16
INT_MIN = np.int32(-2 ** 31)
KEY_NEG_INF = np.int32(np.int64(0x807FFFFF) - 2 ** 32)
NT_DIMS = (((1,), (1,)), ((), ()))


def _const_spec(shape):
    zeros = (0,) * len(shape)
    return pl.BlockSpec(shape, lambda *_: zeros, pipeline_mode=pl.Buffered(1))


def _layer_norm(y, g, b):
    mu = jnp.mean(y, axis=-1, keepdims=True)
    d = y - mu
    var = jnp.mean(d * d, axis=-1, keepdims=True)
    return d * lax.rsqrt(var + LN_EPS) * g + b


def _mod_kernel(c_ref, w_ref, b_ref, o_ref):
    c = c_ref[...]
    a = c * jax.nn.sigmoid(c)
    o_ref[...] = jnp.dot(a, w_ref[...], preferred_element_type=F32,
                         precision=lax.Precision.HIGHEST) + b_ref[...]


def _mod_call(c, w_mod, b_mod):
    bsz, d = c.shape
    n = w_mod.shape[1]
    tn = 1024
    return pl.pallas_call(
        _mod_kernel,
        grid=(n // tn,),
        in_specs=[pl.BlockSpec((bsz, d), lambda j: (0, 0)),
                  pl.BlockSpec((d, tn), lambda j: (0, j)),
                  pl.BlockSpec((1, tn), lambda j: (0, j))],
        out_specs=pl.BlockSpec((bsz, tn), lambda j: (0, j)),
        out_shape=jax.ShapeDtypeStruct((bsz, n), F32),
        compiler_params=pltpu.CompilerParams(dimension_semantics=("arbitrary",),
                                             vmem_limit_bytes=VMEM_LIMIT_BYTES),
        name="mod",
    )(c, w_mod, b_mod.reshape(1, n))


def _proj_kernel(x_ref, mod_ref, tab_ref, wq_ref, wk_ref, wv_ref, wu_ref, wiq_ref, wix_ref,
                 wpool_ref, ps_ref,
                 q_ref, k_ref, v_ref, iq_ref, ik_ref, iw_ref, pool_ref, ubuf_ref, *, ts):
    si = pl.program_id(1)
    m = mod_ref[0]
    shift1 = m[0:1, :]
    scale1 = m[1:2, :]
    h = (x_ref[0] * (1.0 + scale1) + shift1).astype(BF16)

    cos = tab_ref[0]
    sin = tab_ref[1]
    lane = lax.broadcasted_iota(jnp.int32, (ts, LANES), 1)
    first_half = (lane & (HEAD_DIM // 2)) == 0

    def rope(y):
        outs = []
        for j in range(y.shape[1] // LANES):
            yj = y[:, j * LANES:(j + 1) * LANES]
            partner = jnp.where(first_half, pltpu.roll(yj, LANES - HEAD_DIM // 2, 1),
                                pltpu.roll(yj, HEAD_DIM // 2, 1))
            outs.append(yj * cos + partner * sin)
        return outs[0] if len(outs) == 1 else jnp.concatenate(outs, axis=1)

    def proj(w_ref):
        return jnp.dot(h, w_ref[...], preferred_element_type=F32)

    q_ref[0] = rope(proj(wq_ref)).astype(BF16)
    k_ref[0] = rope(proj(wk_ref)).astype(BF16)
    v_ref[0] = proj(wv_ref).astype(BF16)
    iq_ref[0] = rope(proj(wiq_ref)).astype(BF16)
    ix = proj(wix_ref)
    ik_ref[0] = rope(ix[:, :LANES]).astype(BF16)
    iw_ref[0] = ix[:, LANES:] * (IDX_HEADS ** -0.5)

    u = proj(wu_ref)

    @pl.when(si == 0)
    def _():
        ubuf_ref[0:POOL_HALO, :] = jnp.zeros((POOL_HALO, POOL_WIDTH), F32)

    ubuf_ref[POOL_HALO:POOL_HALO + ts, :] = u
    t_pos = si * ts + lax.broadcasted_iota(jnp.int32, (ts, POOL_GROUP_DIM), 0)
    for g, win in enumerate(POOL_WINDOWS):
        lo, hi = g * POOL_GROUP_DIM, (g + 1) * POOL_GROUP_DIM
        ug = u[:, lo:hi]
        acc = ug
        for d in range(1, win):
            acc = acc + ubuf_ref[POOL_HALO - d:POOL_HALO - d + ts, lo:hi]
        cnt = jnp.minimum(t_pos + 1, win).astype(F32)
        pooled = acc / cnt - ug
        po = jnp.dot(pooled.astype(BF16), wpool_ref[g], preferred_element_type=F32) * ps_ref[:, lo:hi]
        pool_ref[0, :, lo:hi] = po.astype(BF16)
    ubuf_ref[0:POOL_HALO, :] = ubuf_ref[ts:ts + POOL_HALO, :]


def _proj_call(x, mod3, tab, wq, wk, wv, wu, wiq, wix, wpool, ps, *, ts):
    bsz, s, d = x.shape
    tok = lambda w: pl.BlockSpec((1, ts, w), lambda b, i: (b, i, 0))
    out_shape = [jax.ShapeDtypeStruct((bsz, s, ATTN_WIDTH), BF16)] * 4 + [
        jax.ShapeDtypeStruct((bsz, s, LANES), BF16),
        jax.ShapeDtypeStruct((bsz, s, LANES), F32),
        jax.ShapeDtypeStruct((bsz, s, POOL_WIDTH), BF16)]
    out_specs = [tok(ATTN_WIDTH)] * 4 + [tok(LANES), tok(LANES), tok(POOL_WIDTH)]
    q, k, v, iq, ik, iw, pool = pl.pallas_call(
        functools.partial(_proj_kernel, ts=ts),
        grid=(bsz, s // ts),
        in_specs=[tok(d),
                  pl.BlockSpec((1, N_MOD, d), lambda b, i: (b, 0, 0)),
                  pl.BlockSpec((2, ts, LANES), lambda b, i: (0, i, 0)),
                  _const_spec(wq.shape), _const_spec(wk.shape), _const_spec(wv.shape),
                  _const_spec(wu.shape), _const_spec(wiq.shape), _const_spec(wix.shape),
                  _const_spec(wpool.shape), _const_spec(ps.shape)],
        out_specs=out_specs,
        out_shape=out_shape,
        scratch_shapes=[pltpu.VMEM((ts + POOL_HALO, POOL_WIDTH), F32)],
        compiler_params=pltpu.CompilerParams(dimension_semantics=("arbitrary", "arbitrary"),
                                             vmem_limit_bytes=VMEM_LIMIT_BYTES),
        name="proj",
    )(x, mod3, tab, wq, wk, wv, wu, wiq, wix, wpool, ps)
    return q, k, v, iq, ik, iw, pool


def _attn_kernel(q_ref, k_ref, v_ref, iq_ref, ik_ref, iw_ref, o_ref,
                 key_ref, bias_ref, qm_ref, iqm_ref, wb_ref, jstar_ref, *, tq, tk, seq, k_sel):
    qi = pl.program_id(1)
    n_chunks = lax.div(qi * tq + tq + tk - 1, tk)
    reps = tk // LANES
    lane = lax.broadcasted_iota(jnp.int32, (tq, LANES), 1)

    def widen(a):
        return jnp.concatenate([a] * reps, axis=1)

    def fold(a):
        out = a[:, 0:LANES]
        for r in range(1, reps):
            out = out + a[:, r * LANES:(r + 1) * LANES]
        return out

    def row_sum(a):
        return jnp.broadcast_to(jnp.sum(a, axis=1, keepdims=True), (tq, LANES))

    for h in range(N_HEADS):
        j = h // 2
        keep = (lane >= HEAD_DIM) if h % 2 else (lane < HEAD_DIM)
        qm_ref[h] = jnp.where(keep, q_ref[0, :, j * LANES:(j + 1) * LANES].astype(F32), 0.0).astype(BF16)
        iqm_ref[h] = jnp.where(keep, iq_ref[0, :, j * LANES:(j + 1) * LANES].astype(F32), 0.0).astype(BF16)
        wb_ref[h] = jnp.broadcast_to(iw_ref[0, :, h:h + 1], (tq, LANES))

    t_idx = qi * tq + lax.broadcasted_iota(jnp.int32, (tq, tk), 0)
    s_iota = lax.broadcasted_iota(jnp.int32, (tq, tk), 1)

    def idx_body(c, carry):
        off = pl.multiple_of(c * tk, tk)
        ikc = ik_ref[0, pl.ds(off, tk), :]
        acc = jnp.zeros((tq, tk), F32)
        for h in range(IDX_HEADS):
            lg = lax.dot_general(iqm_ref[h], ikc, NT_DIMS, preferred_element_type=F32)
            acc = acc + widen(wb_ref[h]) * jnp.maximum(lg, 0.0)
        sc = jnp.where(s_iota + off <= t_idx, acc, -jnp.inf)
        bits = pltpu.bitcast(sc, jnp.int32)
        key_ref[c] = bits ^ ((bits >> 31) & np.int32(0x7FFFFFFF))
        return carry

    lax.fori_loop(0, n_chunks, idx_body, 0)

    def count(pred):
        def body(c, cnt):
            return cnt + fold(jnp.where(pred(key_ref[c], c), 1.0, 0.0))
        return row_sum(lax.fori_loop(0, n_chunks, body, jnp.zeros((tq, LANES), F32)))

    def bit_body(i, cand):
        trial = cand | lax.shift_left(jnp.int32(1), 31 - i)
        trial_w = widen(trial ^ INT_MIN)
        cnt = count(lambda kc, c: kc >= trial_w)
        return jnp.where(cnt >= k_sel, trial, cand)

    cand = lax.fori_loop(0, 32, bit_body, jnp.zeros((tq, LANES), jnp.int32))
    thr = cand ^ INT_MIN
    thr_w = widen(thr)

    cnt_gt = count(lambda kc, c: kc > thr_w)
    cnt_ge = count(lambda kc, c: kc >= thr_w)
    need = k_sel - cnt_gt
    tie = (cnt_ge - cnt_gt > need) & (thr > KEY_NEG_INF)
    jstar_ref[...] = jnp.full((tq, LANES), seq, jnp.int32)

    @pl.when(jnp.max(jnp.where(tie, 1.0, 0.0)) > 0.0)
    def _():
        def jbit_body(i, lo):
            trial = lo | lax.shift_left(jnp.int32(1), (seq.bit_length() - 2) - i)
            trial_w = widen(trial)
            cnt = count(lambda kc, c: (kc == thr_w) & (s_iota + c * tk < trial_w))
            return jnp.where(cnt < need, trial, lo)
        lo = lax.fori_loop(0, seq.bit_length() - 1, jbit_body, jnp.zeros((tq, LANES), jnp.int32))
        jstar_ref[...] = jnp.where(tie, lo, seq)

    jstar_w = widen(jstar_ref[...])

    def bias_body(c, carry):
        kc = key_ref[c]
        s_idx = s_iota + c * tk
        sel = (kc > thr_w) | ((kc == thr_w) & (s_idx <= jstar_w))
        bias_ref[c] = jnp.where(sel & (s_idx <= t_idx), 0.0, -jnp.inf)
        return carry

    lax.fori_loop(0, n_chunks, bias_body, 0)

    outs = []
    for h in range(N_HEADS):
        j = h // 2

        def kv_body(c, carry, h=h, j=j):
            m, l, acc = carry
            off = pl.multiple_of(c * tk, tk)
            kc = k_ref[0, pl.ds(off, tk), j * LANES:(j + 1) * LANES]
            vc = v_ref[0, pl.ds(off, tk), j * LANES:(j + 1) * LANES]
            s = lax.dot_general(qm_ref[h], kc, NT_DIMS, preferred_element_type=F32) + bias_ref[c]
            m_new = jnp.maximum(m, jnp.broadcast_to(jnp.max(s, axis=1, keepdims=True), (tq, LANES)))
            m_safe = jnp.where(m_new == -jnp.inf, 0.0, m_new)
            p = jnp.exp(s - widen(m_safe))
            alpha = jnp.exp(m - m_safe)
            l = alpha * l + row_sum(fold(p))
            acc = alpha * acc + jnp.dot(p.astype(BF16), vc, preferred_element_type=F32)
            return m_new, l, acc

        init = (jnp.full((tq, LANES), -jnp.inf, F32), jnp.zeros((tq, LANES), F32),
                jnp.zeros((tq, LANES), F32))
        _, l, acc = lax.fori_loop(0, n_chunks, kv_body, init)
        outs.append(acc / l)
    for j in range(N_HEADS // 2):
        pair = jnp.where(lane < HEAD_DIM, outs[2 * j], outs[2 * j + 1])
        o_ref[0, :, j * LANES:(j + 1) * LANES] = pair.astype(BF16)


def _attn_call(q, k, v, iq, ik, iw, *, tq, tk):
    bsz, s, _ = q.shape
    k_sel = min(TOPK_MAX, s // 4)
    n_chunks_max = s // tk
    qblk = lambda w: pl.BlockSpec((1, tq, w), lambda b, i: (b, i, 0))
    full = lambda w: pl.BlockSpec((1, s, w), lambda b, i: (b, 0, 0))
    return pl.pallas_call(
        functools.partial(_attn_kernel, tq=tq, tk=tk, seq=s, k_sel=k_sel),
        grid=(bsz, s // tq),
        in_specs=[qblk(ATTN_WIDTH), full(ATTN_WIDTH), full(ATTN_WIDTH), qblk(ATTN_WIDTH),
                  full(LANES), qblk(LANES)],
        out_specs=qblk(ATTN_WIDTH),
        out_shape=jax.ShapeDtypeStruct((bsz, s, ATTN_WIDTH), BF16),
        scratch_shapes=[pltpu.VMEM((n_chunks_max, tq, tk), jnp.int32),
                        pltpu.VMEM((n_chunks_max, tq, tk), F32),
                        pltpu.VMEM((N_HEADS, tq, LANES), BF16),
                        pltpu.VMEM((IDX_HEADS, tq, LANES), BF16),
                        pltpu.VMEM((IDX_HEADS, tq, LANES), F32),
                        pltpu.VMEM((tq, LANES), jnp.int32)],
        compiler_params=pltpu.CompilerParams(dimension_semantics=("arbitrary", "arbitrary"),
                                             vmem_limit_bytes=VMEM_LIMIT_BYTES),
        name="attn",
    )(q, k, v, iq, ik, iw)


def _ffn_kernel(x_ref, attn_ref, pool_ref, mod_ref, wo_ref, wg_ref, wu_ref, wd_ref, ln_ref,
                o_ref, acc_ref, *, alpha, n_ff_chunks):
    m = mod_ref[0]
    gate1, shift2, scale2, gate2 = m[2:3, :], m[3:4, :], m[4:5, :], m[5:6, :]
    ln = ln_ref[...]
    mix = (jnp.dot(attn_ref[0], wo_ref[0:ATTN_WIDTH, :], preferred_element_type=F32)
           + jnp.dot(pool_ref[0], wo_ref[ATTN_WIDTH:, :], preferred_element_type=F32))
    x1 = _layer_norm(alpha * x_ref[0] + gate1 * mix, ln[0:1, :], ln[1:2, :])
    h2 = (x1 * (1.0 + scale2) + shift2).astype(BF16)
    acc_ref[...] = jnp.zeros_like(acc_ref)

    def ff_body(c, carry):
        g = jnp.dot(h2, wg_ref[c], preferred_element_type=F32)
        u = jnp.dot(h2, wu_ref[c], preferred_element_type=F32)
        a = (g * jax.nn.sigmoid(g) * u).astype(BF16)
        acc_ref[...] += jnp.dot(a, wd_ref[c], preferred_element_type=F32)
        return carry

    lax.fori_loop(0, n_ff_chunks, ff_body, 0)
    o_ref[0] = _layer_norm(alpha * x1 + gate2 * acc_ref[...], ln[2:3, :], ln[3:4, :])


def _ffn_call(x, attn, pool, mod3, wo, wg, wu, wd, ln, *, tm, alpha):
    bsz, s, d = x.shape
    tok = lambda w: pl.BlockSpec((1, tm, w), lambda b, i: (b, i, 0))
    return pl.pallas_call(
        functools.partial(_ffn_kernel, alpha=alpha, n_ff_chunks=wg.shape[0]),
        grid=(bsz, s // tm),
        in_specs=[tok(d), tok(ATTN_WIDTH), tok(POOL_WIDTH),
                  pl.BlockSpec((1, N_MOD, d), lambda b, i: (b, 0, 0)),
                  _const_spec(wo.shape), _const_spec(wg.shape), _const_spec(wu.shape),
                  _const_spec(wd.shape), _const_spec(ln.shape)],
        out_specs=tok(d),
        out_shape=jax.ShapeDtypeStruct((bsz, s, d), F32),
        scratch_shapes=[pltpu.VMEM((tm, d), F32)],
        compiler_params=pltpu.CompilerParams(dimension_semantics=("arbitrary", "arbitrary"),
                                             vmem_limit_bytes=VMEM_LIMIT_BYTES),
        name="ffn",
    )(x, attn, pool, mod3, wo, wg, wu, wd, ln)


def _rope_tables(s):
    half = HEAD_DIM // 2
    inv_freq = ROPE_THETA ** (-jnp.arange(0, HEAD_DIM, 2, dtype=F32) / HEAD_DIM)
    ang = jnp.arange(s, dtype=F32)[:, None] * inv_freq[None, :]
    cos, sin = jnp.cos(ang), jnp.sin(ang)
    reps = LANES // half
    cos_t = jnp.tile(cos, (1, reps))
    sin_t = jnp.tile(jnp.concatenate([-sin, sin], axis=1), (1, reps // 2))
    return jnp.stack([cos_t, sin_t])


def _pick_ff_chunk(d_ff):
    for c in (512, 256, 128):
        if d_ff % c == 0:
            return c
    raise ValueError(f"d_ff={d_ff} is not a multiple of {LANES}")


def kernel(x, c, w_mod, b_mod, w_in, w_pool, pool_scale, w_o, ln1_g, ln1_b, w_gate, w_up, w_down,
           ln2_g, ln2_b):
    bsz, s, d = x.shape
    depth = w_mod.shape[0]
    d_ff = w_gate.shape[-1]
    alpha = (2.0 * depth) ** 0.25
    ts = min(512, s)
    tq = min(256, s)
    tk = min(512, s)
    tm = min(512, s)
    ffc = _pick_ff_chunk(d_ff)
    tab = _rope_tables(s)
    a = ATTN_WIDTH
    for l in range(depth):
        mod3 = _mod_call(c, w_mod[l], b_mod[l]).reshape(bsz, N_MOD, d)
        wi = w_in[l]
        wq = (wi[:, 0:a] * HEAD_DIM ** -0.5).astype(BF16)
        wk = wi[:, a:2 * a].astype(BF16)
        wv = wi[:, 2 * a:3 * a].astype(BF16)
        wu = wi[:, 3 * a:3 * a + POOL_WIDTH].astype(BF16)
        o = 3 * a + POOL_WIDTH
        wiq = (wi[:, o:o + IDX_HEADS * IDX_DIM] * IDX_DIM ** -0.5).astype(BF16)
        o += IDX_HEADS * IDX_DIM
        wik = wi[:, o:o + IDX_DIM]
        wiw = wi[:, o + IDX_DIM:o + IDX_DIM + IDX_HEADS]
        wix = jnp.concatenate([wik, wik, wiw, jnp.zeros((d, LANES - IDX_HEADS), F32)], axis=1).astype(BF16)
        q, k, v, iq, ik, iw, pool = _proj_call(
            x, mod3, tab, wq, wk, wv, wu, wiq, wix, w_pool[l].astype(BF16),
            pool_scale[l].reshape(1, POOL_WIDTH), ts=ts)
        attn = _attn_call(q, k, v, iq, ik, iw, tq=tq, tk=tk)
        wg = w_gate[l].reshape(d, d_ff // ffc, ffc).transpose(1, 0, 2).astype(BF16)
        wup = w_up[l].reshape(d, d_ff // ffc, ffc).transpose(1, 0, 2).astype(BF16)
        wd = w_down[l].reshape(d_ff // ffc, ffc, d).astype(BF16)
        ln = jnp.stack([ln1_g[l], ln1_b[l], ln2_g[l], ln2_b[l]])
        x = _ffn_call(x, attn, pool, mod3, w_o[l].astype(BF16), wg, wup, wd, ln, tm=tm, alpha=alpha)
    return x
```

```python
import functools

import numpy as np
import jax
import jax.numpy as jnp
from jax import lax
from jax.experimental import pallas as pl
from jax.experimental.pallas import tpu as pltpu

N_HEADS = 8
HEAD_DIM = 64
HALF_DIM = HEAD_DIM // 2
ATTN_WIDTH = N_HEADS * HEAD_DIM
IDX_HEADS = 8
IDX_DIM = 64
TOPK_MAX = 256
POOL_WINDOWS = (2, 4, 8, 16)
N_POOL_GROUPS = len(POOL_WINDOWS)
POOL_GROUP_DIM = 128
POOL_WIDTH = N_POOL_GROUPS * POOL_GROUP_DIM
ROPE_THETA = 10000.0
LN_EPS = 1e-5
N_MOD = 6

LANES = 128
SUBLANES = 8
POOL_HALO = 16
VMEM_LIMIT_BYTES = 56 * 1024 * 1024

F32 = jnp.float32
BF16 = jnp.bfloat16
INT_MIN = np.int32(-2 ** 31)
KEY_NEG_INF = np.int32(np.int64(0x807FFFFF) - 2 ** 32)


def _const_spec(shape):
    zeros = (0,) * len(shape)
    return pl.BlockSpec(shape, lambda *_: zeros, pipeline_mode=pl.Buffered(1))


def _layer_norm(y, g, b):
    mu = jnp.mean(y, axis=-1, keepdims=True)
    d = y - mu
    var = jnp.mean(d * d, axis=-1, keepdims=True)
    return d * lax.rsqrt(var + LN_EPS) * g + b


def _mod_kernel(c_ref, w_ref, b_ref, o_ref):
    c = c_ref[...]
    a = c * jax.nn.sigmoid(c)
    o_ref[...] = jnp.dot(a, w_ref[...], preferred_element_type=F32,
                         precision=lax.Precision.HIGHEST) + b_ref[...]


def _mod_call(c, w_mod, b_mod):
    bsz, d = c.shape
    n = w_mod.shape[1]
    tn = 1024
    return pl.pallas_call(
        _mod_kernel,
        grid=(n // tn,),
        in_specs=[pl.BlockSpec((bsz, d), lambda j: (0, 0)),
                  pl.BlockSpec((d, tn), lambda j: (0, j)),
                  pl.BlockSpec((1, tn), lambda j: (0, j))],
        out_specs=pl.BlockSpec((bsz, tn), lambda j: (0, j)),
        out_shape=jax.ShapeDtypeStruct((bsz, n), F32),
        compiler_params=pltpu.CompilerParams(dimension_semantics=("arbitrary",),
                                             vmem_limit_bytes=VMEM_LIMIT_BYTES),
        name="mod",
    )(c, w_mod, b_mod.reshape(1, n))


def _proj_kernel(x_ref, mod_ref, tab_ref, tabt_ref, wq_ref, wk_ref, wv_ref, wu_ref, wiq_ref, wix_ref,
                 wpool_ref, ps_ref,
                 qt_ref, k_ref, vt_ref, iqt_ref, ik_ref, wt_ref, pool_ref, ubuf_ref, *, ts, tk):
    si = pl.program_id(1)
    m = mod_ref[0]
    shift1 = m[0:1, :]
    scale1 = m[1:2, :]
    h = (x_ref[0] * (1.0 + scale1) + shift1).astype(BF16)

    cos = tab_ref[0]
    sin = tab_ref[1]
    lane = lax.broadcasted_iota(jnp.int32, (ts, LANES), 1)
    first_half = (lane & HALF_DIM) == 0

    def rope(y):
        outs = []
        for j in range(y.shape[1] // LANES):
            yj = y[:, j * LANES:(j + 1) * LANES]
            partner = jnp.where(first_half, pltpu.roll(yj, LANES - HALF_DIM, 1),
                                pltpu.roll(yj, HALF_DIM, 1))
            outs.append(yj * cos + partner * sin)
        return outs[0] if len(outs) == 1 else jnp.concatenate(outs, axis=1)

    cos_t = tabt_ref[0]
    sin_t = tabt_ref[1]

    def rope_t(yt):
        outs = []
        for hd in range(yt.shape[0] // HEAD_DIM):
            x1 = yt[hd * HEAD_DIM:hd * HEAD_DIM + HALF_DIM, :]
            x2 = yt[hd * HEAD_DIM + HALF_DIM:(hd + 1) * HEAD_DIM, :]
            outs.append(x1 * cos_t - x2 * sin_t)
            outs.append(x2 * cos_t + x1 * sin_t)
        return jnp.concatenate(outs, axis=0)

    def proj(w_ref):
        return jnp.dot(h, w_ref[...], preferred_element_type=F32)

    qt_ref[0] = rope_t(proj(wq_ref).T).astype(BF16)
    k_ref[0] = rope(proj(wk_ref)).astype(BF16)
    vt = proj(wv_ref).T.astype(BF16)
    for cc in range(ts // tk):
        vt_ref[0, cc] = vt[:, cc * tk:(cc + 1) * tk]
    iqt_ref[0] = rope_t(proj(wiq_ref).T).astype(BF16)
    ix = proj(wix_ref)
    ik_ref[0] = rope(ix[:, :LANES]).astype(BF16)
    wt_ref[0] = (ix[:, LANES:].T)[0:IDX_HEADS, :] * (IDX_HEADS ** -0.5)

    u = proj(wu_ref)

    @pl.when(si == 0)
    def _():
        ubuf_ref[0:POOL_HALO, :] = jnp.zeros((POOL_HALO, POOL_WIDTH), F32)

    ubuf_ref[POOL_HALO:POOL_HALO + ts, :] = u
    t_pos = si * ts + lax.broadcasted_iota(jnp.int32, (ts, POOL_GROUP_DIM), 0)
    for g, win in enumerate(POOL_WINDOWS):
        lo, hi = g * POOL_GROUP_DIM, (g + 1) * POOL_GROUP_DIM
        ug = u[:, lo:hi]
        acc = ug
        for d in range(1, win):
            acc = acc + ubuf_ref[POOL_HALO - d:POOL_HALO - d + ts, lo:hi]
        cnt = jnp.minimum(t_pos + 1, win).astype(F32)
        pooled = acc / cnt - ug
        po = jnp.dot(pooled.astype(BF16), wpool_ref[g], preferred_element_type=F32) * ps_ref[:, lo:hi]
        pool_ref[0, :, lo:hi] = po.astype(BF16)
    ubuf_ref[0:POOL_HALO, :] = ubuf_ref[ts:ts + POOL_HALO, :]


def _proj_call(x, mod3, tab, tabt, wq, wk, wv, wu, wiq, wix, wpool, ps, *, ts, tk):
    bsz, s, d = x.shape
    tok = lambda w: pl.BlockSpec((1, ts, w), lambda b, i: (b, i, 0))
    feat = lambda r: pl.BlockSpec((1, r, ts), lambda b, i: (b, 0, i))
    cpt = ts // tk
    out_shape = [jax.ShapeDtypeStruct((bsz, ATTN_WIDTH, s), BF16),
                 jax.ShapeDtypeStruct((bsz, s, ATTN_WIDTH), BF16),
                 jax.ShapeDtypeStruct((bsz, s // tk, ATTN_WIDTH, tk), BF16),
                 jax.ShapeDtypeStruct((bsz, ATTN_WIDTH, s), BF16),
                 jax.ShapeDtypeStruct((bsz, s, LANES), BF16),
                 jax.ShapeDtypeStruct((bsz, IDX_HEADS, s), F32),
                 jax.ShapeDtypeStruct((bsz, s, POOL_WIDTH), BF16)]
    out_specs = [feat(ATTN_WIDTH), tok(ATTN_WIDTH),
                 pl.BlockSpec((1, cpt, ATTN_WIDTH, tk), lambda b, i: (b, i, 0, 0)),
                 feat(ATTN_WIDTH), tok(LANES), feat(IDX_HEADS), tok(POOL_WIDTH)]
    return pl.pallas_call(
        functools.partial(_proj_kernel, ts=ts, tk=tk),
        grid=(bsz, s // ts),
        in_specs=[tok(d),
                  pl.BlockSpec((1, N_MOD, d), lambda b, i: (b, 0, 0)),
                  pl.BlockSpec((2, ts, LANES), lambda b, i: (0, i, 0)),
                  pl.BlockSpec((2, HALF_DIM, ts), lambda b, i: (0, 0, i)),
                  _const_spec(wq.shape), _const_spec(wk.shape), _const_spec(wv.shape),
                  _const_spec(wu.shape), _const_spec(wiq.shape), _const_spec(wix.shape),
                  _const_spec(wpool.shape), _const_spec(ps.shape)],
        out_specs=out_specs,
        out_shape=out_shape,
        scratch_shapes=[pltpu.VMEM((ts + POOL_HALO, POOL_WIDTH), F32)],
        compiler_params=pltpu.CompilerParams(dimension_semantics=("arbitrary", "arbitrary"),
                                             vmem_limit_bytes=VMEM_LIMIT_BYTES),
        name="proj",
    )(x, mod3, tab, tabt, wq, wk, wv, wu, wiq, wix, wpool, ps)


def _key_to_float(key):
    key = jnp.maximum(key, KEY_NEG_INF)
    bits = key ^ ((key >> 31) & np.int32(0x7FFFFFFF))
    return pltpu.bitcast(bits, F32)


def _fold_rows(x, op):
    while x.shape[0] > SUBLANES:
        half = x.shape[0] // 2
        x = op(x[:half], x[half:])
    return x


def _attn_kernel(qt_ref, k_ref, vt_ref, iqt_ref, ik_ref, wt_ref, o_ref,
                 sc_ref, bias_ref, s_ref, acc_ref, qm_ref, iqm_ref, jstar_ref, *, tq, tk, seq, k_sel):
    qi = pl.program_id(1)
    n_chunks = lax.div(qi * tq + tq + tk - 1, tk)

    zeros_half = jnp.zeros((HEAD_DIM, tq), BF16)
    for h in range(N_HEADS):
        lo = h * HEAD_DIM
        pad = (lambda own: [own, zeros_half]) if h % 2 == 0 else (lambda own: [zeros_half, own])
        iqm_ref[h] = jnp.concatenate(pad(iqt_ref[0, lo:lo + HEAD_DIM, :]), axis=0)
        qm_ref[h // 2, :, (h % 2) * tq:(h % 2 + 1) * tq] = jnp.concatenate(
            pad(qt_ref[0, lo:lo + HEAD_DIM, :]), axis=0)

    t_idx = qi * tq + lax.broadcasted_iota(jnp.int32, (tk, tq), 1)
    s_iota = lax.broadcasted_iota(jnp.int32, (tk, tq), 0)

    def idx_body(c, carry):
        off = pl.multiple_of(c * tk, tk)
        ikc = ik_ref[0, pl.ds(off, tk), :]
        acc = jnp.zeros((tk, tq), F32)
        for h in range(IDX_HEADS):
            lg = jnp.dot(ikc, iqm_ref[h], preferred_element_type=F32)
            acc = acc + wt_ref[0, h:h + 1, :] * jnp.maximum(lg, 0.0)
        sc_ref[c] = jnp.where(s_iota + off <= t_idx, acc, -jnp.inf)
        return carry

    lax.fori_loop(0, n_chunks, idx_body, 0)

    def count(pred):
        def body(c, cnt):
            return cnt + _fold_rows(jnp.where(pred(sc_ref[c], c), 1.0, 0.0), jnp.add)
        cnt = lax.fori_loop(0, n_chunks, body, jnp.zeros((SUBLANES, tq), F32))
        return jnp.sum(cnt, axis=0, keepdims=True)

    def bit_body(i, cand):
        trial = cand | lax.shift_left(jnp.int32(1), 31 - i)
        trial_f = _key_to_float(trial ^ INT_MIN)
        cnt = count(lambda sc, c: sc >= trial_f)
        return jnp.where(cnt >= k_sel, trial, cand)

    cand = lax.fori_loop(0, 32, bit_body, jnp.zeros((1, tq), jnp.int32))
    thr = _key_to_float(cand ^ INT_MIN)

    cnt_gt = count(lambda sc, c: sc > thr)
    cnt_ge = count(lambda sc, c: sc >= thr)
    need = k_sel - cnt_gt
    tie = (cnt_ge - cnt_gt > need) & (thr > -jnp.inf)
    jstar_ref[...] = jnp.full((1, tq), seq, jnp.int32)

    @pl.when(jnp.max(jnp.where(tie, 1.0, 0.0)) > 0.0)
    def _():
        top_bit = seq.bit_length() - 2

        def jbit_body(i, lo):
            trial = lo | lax.shift_left(jnp.int32(1), top_bit - i)
            cnt = count(lambda sc, c: (sc == thr) & (s_iota + c * tk < trial))
            return jnp.where(cnt < need, trial, lo)

        lo = lax.fori_loop(0, top_bit + 1, jbit_body, jnp.zeros((1, tq), jnp.int32))
        jstar_ref[...] = jnp.where(tie, lo, seq)

    jstar = jstar_ref[...]

    def bias_body(c, carry):
        sc = sc_ref[c]
        s_idx = s_iota + c * tk
        sel = (sc > thr) | ((sc == thr) & (s_idx <= jstar))
        bias_ref[c] = jnp.where(sel & (s_idx <= t_idx), 0.0, -jnp.inf)
        return carry

    lax.fori_loop(0, n_chunks, bias_body, 0)

    n_pairs = N_HEADS // 2

    def score_body(c, mx):
        off = pl.multiple_of(c * tk, tk)
        bias = bias_ref[c]
        bias2 = jnp.concatenate([bias, bias], axis=1)
        new = []
        for j in range(n_pairs):
            kc = k_ref[0, pl.ds(off, tk), j * LANES:(j + 1) * LANES]
            s = jnp.dot(kc, qm_ref[j], preferred_element_type=F32) + bias2
            s_ref[j, c] = s
            new.append(jnp.maximum(mx[j], _fold_rows(s, jnp.maximum)))
        return tuple(new)

    mx = lax.fori_loop(0, n_chunks, score_body,
                       (jnp.full((SUBLANES, 2 * tq), -jnp.inf, F32),) * n_pairs)
    m = [jnp.max(mx[j], axis=0, keepdims=True) for j in range(n_pairs)]
    acc_ref[...] = jnp.zeros_like(acc_ref)

    def pv_body(c, l):
        new = []
        for j in range(n_pairs):
            p = jnp.exp(s_ref[j, c] - m[j])
            vc = vt_ref[0, c, j * LANES:(j + 1) * LANES, :]
            acc_ref[j] += jnp.dot(vc, p.astype(BF16), preferred_element_type=F32)
            new.append(l[j] + _fold_rows(p, jnp.add))
        return tuple(new)

    l = lax.fori_loop(0, n_chunks, pv_body, (jnp.zeros((SUBLANES, 2 * tq), F32),) * n_pairs)
    out_rows = []
    for j in range(n_pairs):
        o = acc_ref[j] / jnp.sum(l[j], axis=0, keepdims=True)
        out_rows.append(o[0:HEAD_DIM, 0:tq])
        out_rows.append(o[HEAD_DIM:LANES, tq:2 * tq])
    o_ref[0] = jnp.concatenate(out_rows, axis=0).T.astype(BF16)


def _attn_call(qt, k, vt, iqt, ik, wt, *, tq, tk):
    bsz, s, _ = k.shape
    k_sel = min(TOPK_MAX, s // 4)
    nc = s // tk
    assert s & (s - 1) == 0 and vt.shape == (bsz, nc, ATTN_WIDTH, tk)
    qfeat = lambda r: pl.BlockSpec((1, r, tq), lambda b, i: (b, 0, i))
    full = lambda w: pl.BlockSpec((1, s, w), lambda b, i: (b, 0, 0))
    return pl.pallas_call(
        functools.partial(_attn_kernel, tq=tq, tk=tk, seq=s, k_sel=k_sel),
        grid=(bsz, s // tq),
        in_specs=[qfeat(ATTN_WIDTH), full(ATTN_WIDTH),
                  pl.BlockSpec((1, nc, ATTN_WIDTH, tk), lambda b, i: (b, 0, 0, 0)),
                  qfeat(ATTN_WIDTH), full(LANES), qfeat(IDX_HEADS)],
        out_specs=pl.BlockSpec((1, tq, ATTN_WIDTH), lambda b, i: (b, i, 0)),
        out_shape=jax.ShapeDtypeStruct((bsz, s, ATTN_WIDTH), BF16),
        scratch_shapes=[pltpu.VMEM((nc, tk, tq), F32),
                        pltpu.VMEM((nc, tk, tq), F32),
                        pltpu.VMEM((N_HEADS // 2, nc, tk, 2 * tq), F32),
                        pltpu.VMEM((N_HEADS // 2, LANES, 2 * tq), F32),
                        pltpu.VMEM((N_HEADS // 2, LANES, 2 * tq), BF16),
                        pltpu.VMEM((IDX_HEADS, LANES, tq), BF16),
                        pltpu.VMEM((1, tq), jnp.int32)],
        compiler_params=pltpu.CompilerParams(dimension_semantics=("arbitrary", "arbitrary"),
                                             vmem_limit_bytes=VMEM_LIMIT_BYTES),
        name="attn",
    )(qt, k, vt, iqt, ik, wt)


def _ffn_kernel(x_ref, attn_ref, pool_ref, mod_ref, wo_ref, wg_ref, wu_ref, wd_ref, ln_ref,
                o_ref, acc_ref, *, alpha, n_ff_chunks):
    m = mod_ref[0]
    gate1, shift2, scale2, gate2 = m[2:3, :], m[3:4, :], m[4:5, :], m[5:6, :]
    ln = ln_ref[...]
    mix = (jnp.dot(attn_ref[0], wo_ref[0:ATTN_WIDTH, :], preferred_element_type=F32)
           + jnp.dot(pool_ref[0], wo_ref[ATTN_WIDTH:, :], preferred_element_type=F32))
    x1 = _layer_norm(alpha * x_ref[0] + gate1 * mix, ln[0:1, :], ln[1:2, :])
    h2 = (x1 * (1.0 + scale2) + shift2).astype(BF16)
    acc_ref[...] = jnp.zeros_like(acc_ref)

    def ff_body(c, carry):
        g = jnp.dot(h2, wg_ref[c], preferred_element_type=F32)
        u = jnp.dot(h2, wu_ref[c], preferred_element_type=F32)
        a = (g * jax.nn.sigmoid(g) * u).astype(BF16)
        acc_ref[...] += jnp.dot(a, wd_ref[c], preferred_element_type=F32)
        return carry

    lax.fori_loop(0, n_ff_chunks, ff_body, 0)
    o_ref[0] = _layer_norm(alpha * x1 + gate2 * acc_ref[...], ln[2:3, :], ln[3:4, :])


def _ffn_call(x, attn, pool, mod3, wo, wg, wu, wd, ln, *, tm, alpha):
    bsz, s, d = x.shape
    tok = lambda w: pl.BlockSpec((1, tm, w), lambda b, i: (b, i, 0))
    return pl.pallas_call(
        functools.partial(_ffn_kernel, alpha=alpha, n_ff_chunks=wg.shape[0]),
        grid=(bsz, s // tm),
        in_specs=[tok(d), tok(ATTN_WIDTH), tok(POOL_WIDTH),
                  pl.BlockSpec((1, N_MOD, d), lambda b, i: (b, 0, 0)),
                  _const_spec(wo.shape), _const_spec(wg.shape), _const_spec(wu.shape),
                  _const_spec(wd.shape), _const_spec(ln.shape)],
        out_specs=tok(d),
        out_shape=jax.ShapeDtypeStruct((bsz, s, d), F32),
        scratch_shapes=[pltpu.VMEM((tm, d), F32)],
        compiler_params=pltpu.CompilerParams(dimension_semantics=("arbitrary", "arbitrary"),
                                             vmem_limit_bytes=VMEM_LIMIT_BYTES),
        name="ffn",
    )(x, attn, pool, mod3, wo, wg, wu, wd, ln)


def _rope_tables(s):
    inv_freq = ROPE_THETA ** (-jnp.arange(0, HEAD_DIM, 2, dtype=F32) / HEAD_DIM)
    ang = jnp.arange(s, dtype=F32)[:, None] * inv_freq[None, :]
    cos, sin = jnp.cos(ang), jnp.sin(ang)
    reps = LANES // HALF_DIM
    cos_l = jnp.tile(cos, (1, reps))
    sin_l = jnp.tile(jnp.concatenate([-sin, sin], axis=1), (1, reps // 2))
    return jnp.stack([cos_l, sin_l]), jnp.stack([cos.T, sin.T])


def _pick_ff_chunk(d_ff):
    for c in (512, 256, 128):
        if d_ff % c == 0:
            return c
    raise ValueError(f"d_ff={d_ff} is not a multiple of {LANES}")


def kernel(x, c, w_mod, b_mod, w_in, w_pool, pool_scale, w_o, ln1_g, ln1_b, w_gate, w_up, w_down,
           ln2_g, ln2_b):
    bsz, s, d = x.shape
    depth = w_mod.shape[0]
    d_ff = w_gate.shape[-1]
    alpha = (2.0 * depth) ** 0.25
    ts = min(512, s)
    tq = min(256, s)
    tk = min(512, s)
    tm = min(512, s)
    ffc = _pick_ff_chunk(d_ff)
    tab, tabt = _rope_tables(s)
    a = ATTN_WIDTH
    for l in range(depth):
        mod3 = _mod_call(c, w_mod[l], b_mod[l]).reshape(bsz, N_MOD, d)
        wi = w_in[l]
        wq = (wi[:, 0:a] * HEAD_DIM ** -0.5).astype(BF16)
        wk = wi[:, a:2 * a].astype(BF16)
        wv = wi[:, 2 * a:3 * a].astype(BF16)
        wu = wi[:, 3 * a:3 * a + POOL_WIDTH].astype(BF16)
        o = 3 * a + POOL_WIDTH
        wiq = (wi[:, o:o + IDX_HEADS * IDX_DIM] * IDX_DIM ** -0.5).astype(BF16)
        o += IDX_HEADS * IDX_DIM
        wik = wi[:, o:o + IDX_DIM]
        wiw = wi[:, o + IDX_DIM:o + IDX_DIM + IDX_HEADS]
        wix = jnp.concatenate([wik, wik, wiw, jnp.zeros((d, LANES - IDX_HEADS), F32)], axis=1).astype(BF16)
        qt, k, vt, iqt, ik, wt, pool = _proj_call(
            x, mod3, tab, tabt, wq, wk, wv, wu, wiq, wix, w_pool[l].astype(BF16),
            pool_scale[l].reshape(1, POOL_WIDTH), ts=ts, tk=tk)
        attn = _attn_call(qt, k, vt, iqt, ik, wt, tq=tq, tk=tk)
        wg = w_gate[l].reshape(d, d_ff // ffc, ffc).transpose(1, 0, 2).astype(BF16)
        wup = w_up[l].reshape(d, d_ff // ffc, ffc).transpose(1, 0, 2).astype(BF16)
        wd = w_down[l].reshape(d_ff // ffc, ffc, d).astype(BF16)
        ln = jnp.stack([ln1_g[l], ln1_b[l], ln2_g[l], ln2_b[l]])
        x = _ffn_call(x, attn, pool, mod3, w_o[l].astype(BF16), wg, wup, wd, ln, tm=tm, alpha=alpha)
    return x
```

```python
import functools

import numpy as np
import jax
import jax.numpy as jnp
from jax import lax
from jax.experimental import pallas as pl
from jax.experimental.pallas import tpu as pltpu

N_HEADS = 8
HEAD_DIM = 64
HALF_DIM = HEAD_DIM // 2
ATTN_WIDTH = N_HEADS * HEAD_DIM
IDX_HEADS = 8
IDX_DIM = 64
TOPK_MAX = 256
POOL_WINDOWS = (2, 4, 8, 16)
N_POOL_GROUPS = len(POOL_WINDOWS)
POOL_GROUP_DIM = 128
POOL_WIDTH = N_POOL_GROUPS * POOL_GROUP_DIM
ROPE_THETA = 10000.0
LN_EPS = 1e-5
N_MOD = 6
LOG2_E = 1.4426950408889634

LANES = 128
SUBLANES = 8
POOL_HALO = 16
VMEM_LIMIT_BYTES = 56 * 1024 * 1024

F32 = jnp.float32
BF16 = jnp.bfloat16
INT_MIN = np.int32(-2 ** 31)
KEY_NEG_INF = np.int32(np.int64(0x807FFFFF) - 2 ** 32)


def _const_spec(shape):
    zeros = (0,) * len(shape)
    return pl.BlockSpec(shape, lambda *_: zeros, pipeline_mode=pl.Buffered(1))


def _layer_norm(y, g, b):
    mu = jnp.mean(y, axis=-1, keepdims=True)
    d = y - mu
    var = jnp.mean(d * d, axis=-1, keepdims=True)
    return d * lax.rsqrt(var + LN_EPS) * g + b


def _mod_kernel(c_ref, w_ref, b_ref, o_ref):
    c = c_ref[...]
    a = c * jax.nn.sigmoid(c)
    o_ref[...] = jnp.dot(a, w_ref[...], preferred_element_type=F32,
                         precision=lax.Precision.HIGHEST) + b_ref[...]


def _mod_call(c, w_mod, b_mod):
    bsz, d = c.shape
    n = w_mod.shape[1]
    tn = 1024
    return pl.pallas_call(
        _mod_kernel,
        grid=(n // tn,),
        in_specs=[pl.BlockSpec((bsz, d), lambda j: (0, 0)),
                  pl.BlockSpec((d, tn), lambda j: (0, j)),
                  pl.BlockSpec((1, tn), lambda j: (0, j))],
        out_specs=pl.BlockSpec((bsz, tn), lambda j: (0, j)),
        out_shape=jax.ShapeDtypeStruct((bsz, n), F32),
        compiler_params=pltpu.CompilerParams(dimension_semantics=("arbitrary",),
                                             vmem_limit_bytes=VMEM_LIMIT_BYTES),
        name="mod",
    )(c, w_mod, b_mod.reshape(1, n))


def _proj_kernel(x_ref, mod_ref, tab_ref, tabt_ref, wq_ref, wk_ref, wv_ref, wu_ref, wiq_ref, wix_ref,
                 wpool_ref, ps_ref,
                 qt_ref, k_ref, vt_ref, iqt_ref, ik_ref, wt_ref, pool_ref, ubuf_ref, *, ts, tk):
    si = pl.program_id(1)
    m = mod_ref[0]
    shift1 = m[0:1, :]
    scale1 = m[1:2, :]
    h = (x_ref[0] * (1.0 + scale1) + shift1).astype(BF16)

    cos = tab_ref[0]
    sin = tab_ref[1]
    lane = lax.broadcasted_iota(jnp.int32, (ts, LANES), 1)
    first_half = (lane & HALF_DIM) == 0

    def rope(y):
        outs = []
        for j in range(y.shape[1] // LANES):
            yj = y[:, j * LANES:(j + 1) * LANES]
            partner = jnp.where(first_half, pltpu.roll(yj, LANES - HALF_DIM, 1),
                                pltpu.roll(yj, HALF_DIM, 1))
            outs.append(yj * cos + partner * sin)
        return outs[0] if len(outs) == 1 else jnp.concatenate(outs, axis=1)

    cos_t = tabt_ref[0]
    sin_t = tabt_ref[1]

    def rope_t(yt):
        outs = []
        for hd in range(yt.shape[0] // HEAD_DIM):
            x1 = yt[hd * HEAD_DIM:hd * HEAD_DIM + HALF_DIM, :]
            x2 = yt[hd * HEAD_DIM + HALF_DIM:(hd + 1) * HEAD_DIM, :]
            outs.append(x1 * cos_t - x2 * sin_t)
            outs.append(x2 * cos_t + x1 * sin_t)
        return jnp.concatenate(outs, axis=0)

    def proj(w_ref):
        return jnp.dot(h, w_ref[...], preferred_element_type=F32)

    qt_ref[0] = rope_t(proj(wq_ref).T).astype(BF16)
    k_ref[0] = rope(proj(wk_ref)).astype(BF16)
    vt = proj(wv_ref).T.astype(BF16)
    for cc in range(ts // tk):
        vt_ref[0, cc] = vt[:, cc * tk:(cc + 1) * tk]
    iqt_ref[0] = rope_t(proj(wiq_ref).T).astype(BF16)
    ix = proj(wix_ref)
    ik_ref[0] = rope(ix[:, :LANES]).astype(BF16)
    wt_ref[0] = (ix[:, LANES:].T)[0:IDX_HEADS, :] * (IDX_HEADS ** -0.5)

    u = proj(wu_ref)

    @pl.when(si == 0)
    def _():
        ubuf_ref[0:POOL_HALO, :] = jnp.zeros((POOL_HALO, POOL_WIDTH), F32)

    ubuf_ref[POOL_HALO:POOL_HALO + ts, :] = u
    t_pos = si * ts + lax.broadcasted_iota(jnp.int32, (ts, POOL_GROUP_DIM), 0)
    for g, win in enumerate(POOL_WINDOWS):
        lo, hi = g * POOL_GROUP_DIM, (g + 1) * POOL_GROUP_DIM
        ug = u[:, lo:hi]
        acc = ug
        for d in range(1, win):
            acc = acc + ubuf_ref[POOL_HALO - d:POOL_HALO - d + ts, lo:hi]
        cnt = jnp.minimum(t_pos + 1, win).astype(F32)
        pooled = acc / cnt - ug
        po = jnp.dot(pooled.astype(BF16), wpool_ref[g], preferred_element_type=F32) * ps_ref[:, lo:hi]
        pool_ref[0, :, lo:hi] = po.astype(BF16)
    ubuf_ref[0:POOL_HALO, :] = ubuf_ref[ts:ts + POOL_HALO, :]


def _proj_call(x, mod3, tab, tabt, wq, wk, wv, wu, wiq, wix, wpool, ps, *, ts, tk):
    bsz, s, d = x.shape
    tok = lambda w: pl.BlockSpec((1, ts, w), lambda b, i: (b, i, 0))
    feat = lambda r: pl.BlockSpec((1, r, ts), lambda b, i: (b, 0, i))
    cpt = ts // tk
    out_shape = [jax.ShapeDtypeStruct((bsz, ATTN_WIDTH, s), BF16),
                 jax.ShapeDtypeStruct((bsz, s, ATTN_WIDTH), BF16),
                 jax.ShapeDtypeStruct((bsz, s // tk, ATTN_WIDTH, tk), BF16),
                 jax.ShapeDtypeStruct((bsz, ATTN_WIDTH, s), BF16),
                 jax.ShapeDtypeStruct((bsz, s, LANES), BF16),
                 jax.ShapeDtypeStruct((bsz, IDX_HEADS, s), F32),
                 jax.ShapeDtypeStruct((bsz, s, POOL_WIDTH), BF16)]
    out_specs = [feat(ATTN_WIDTH), tok(ATTN_WIDTH),
                 pl.BlockSpec((1, cpt, ATTN_WIDTH, tk), lambda b, i: (b, i, 0, 0)),
                 feat(ATTN_WIDTH), tok(LANES), feat(IDX_HEADS), tok(POOL_WIDTH)]
    return pl.pallas_call(
        functools.partial(_proj_kernel, ts=ts, tk=tk),
        grid=(bsz, s // ts),
        in_specs=[tok(d),
                  pl.BlockSpec((1, N_MOD, d), lambda b, i: (b, 0, 0)),
                  pl.BlockSpec((2, ts, LANES), lambda b, i: (0, i, 0)),
                  pl.BlockSpec((2, HALF_DIM, ts), lambda b, i: (0, 0, i)),
                  _const_spec(wq.shape), _const_spec(wk.shape), _const_spec(wv.shape),
                  _const_spec(wu.shape), _const_spec(wiq.shape), _const_spec(wix.shape),
                  _const_spec(wpool.shape), _const_spec(ps.shape)],
        out_specs=out_specs,
        out_shape=out_shape,
        scratch_shapes=[pltpu.VMEM((ts + POOL_HALO, POOL_WIDTH), F32)],
        compiler_params=pltpu.CompilerParams(dimension_semantics=("arbitrary", "arbitrary"),
                                             vmem_limit_bytes=VMEM_LIMIT_BYTES),
        name="proj",
    )(x, mod3, tab, tabt, wq, wk, wv, wu, wiq, wix, wpool, ps)


def _key_to_float(key):
    key = jnp.maximum(key, KEY_NEG_INF)
    bits = key ^ ((key >> 31) & np.int32(0x7FFFFFFF))
    return pltpu.bitcast(bits, F32)


def _ukey16_to_float(u):
    u = jnp.maximum(u, UKEY16_NEG_INF)
    bits16 = jnp.where(u >= 0x8000, u ^ 0x8000, u ^ 0xFFFF)
    return pltpu.bitcast(lax.shift_left(bits16, 16), F32)


FOLD_CHAINS = 4
BF16_ROWS = 16
BF16_EXACT_INT = 256
UKEY16_NEG_INF = 0x007F
COARSE_BELOW = 0x8001
COARSE_ABOVE = 0x10001
FINE_STEPS = (COARSE_BELOW + COARSE_ABOVE).bit_length()


def _fold_rows(x, op, rows=SUBLANES):
    blocks = [x[r:r + rows] for r in range(0, x.shape[0], rows)]
    chains = blocks[:FOLD_CHAINS]
    for i, blk in enumerate(blocks[FOLD_CHAINS:]):
        chains[i % FOLD_CHAINS] = op(chains[i % FOLD_CHAINS], blk)
    while len(chains) > 1:
        chains = [op(chains[i], chains[i + 1]) for i in range(0, len(chains) - 1, 2)] + (
            [chains[-1]] if len(chains) % 2 else [])
    return chains[0]


def _attn_kernel(qt_ref, k_ref, vt_ref, iqt_ref, ik_ref, wt_ref, o_ref,
                 sc_ref, scb_ref, bias_ref, s_ref, acc_ref, qm_ref, iqm_ref, jstar_ref, *, tq, tk, seq, k_sel):
    qi = pl.program_id(1)
    n_chunks = lax.div(qi * tq + tq + tk - 1, tk)

    zeros_half = jnp.zeros((HEAD_DIM, tq), BF16)
    for h in range(N_HEADS):
        lo = h * HEAD_DIM
        pad = (lambda own: [own, zeros_half]) if h % 2 == 0 else (lambda own: [zeros_half, own])
        iqm_ref[h] = jnp.concatenate(pad(iqt_ref[0, lo:lo + HEAD_DIM, :]), axis=0)
        qm_ref[h // 2, :, (h % 2) * tq:(h % 2 + 1) * tq] = jnp.concatenate(
            pad(qt_ref[0, lo:lo + HEAD_DIM, :]), axis=0)

    t_idx = qi * tq + lax.broadcasted_iota(jnp.int32, (tk, tq), 1)
    s_iota = lax.broadcasted_iota(jnp.int32, (tk, tq), 0)

    def idx_body(c, carry):
        off = pl.multiple_of(c * tk, tk)
        ikc = ik_ref[0, pl.ds(off, tk), :]
        acc = jnp.zeros((tk, tq), F32)
        for h in range(IDX_HEADS):
            lg = jnp.dot(ikc, iqm_ref[h], preferred_element_type=F32)
            acc = acc + wt_ref[0, h:h + 1, :] * jnp.maximum(lg, 0.0)
        sc = jnp.where(s_iota + off <= t_idx, acc, -jnp.inf)
        sc_ref[c] = sc
        scb_ref[c] = sc.astype(BF16)
        return carry

    lax.fori_loop(0, n_chunks, idx_body, 0)

    def count(pred):
        def body(c, cnt):
            return cnt + _fold_rows(jnp.where(pred(sc_ref[c], c), 1.0, 0.0), jnp.add)
        cnt = lax.fori_loop(0, n_chunks, body, jnp.zeros((SUBLANES, tq), F32))
        return jnp.sum(cnt, axis=0, keepdims=True)

    def count_coarse(trial_b):
        one, zero = jnp.ones((), BF16), jnp.zeros((), BF16)

        def body(c, cnt):
            return cnt + _fold_rows(jnp.where(scb_ref[c] >= trial_b, one, zero), jnp.add, BF16_ROWS)
        cnt = lax.fori_loop(0, n_chunks, body, jnp.zeros((BF16_ROWS, tq), BF16))
        return jnp.sum(cnt.astype(F32), axis=0, keepdims=True)

    def coarse_body(i, cand):
        trial = cand | lax.shift_left(jnp.int32(1), 15 - i)
        cnt = count_coarse(_ukey16_to_float(trial).astype(BF16))
        return jnp.where(cnt >= k_sel, trial, cand)

    cand16 = lax.fori_loop(0, 16, coarse_body, jnp.zeros((1, tq), jnp.int32))
    cand16 = jnp.maximum(cand16, UKEY16_NEG_INF)
    center = (lax.shift_left(cand16, 16) | jnp.where(cand16 >= 0x8000, 0, 0xFFFF)) ^ INT_MIN

    def fine_body(i, carry):
        lo, hi = carry
        mid = lo + lax.shift_right_logical(hi - lo, 1)
        feasible = count(lambda sc, c: sc >= _key_to_float(mid)) >= k_sel
        return jnp.where(feasible, mid, lo), jnp.where(feasible, hi, mid)

    lo, _ = lax.fori_loop(0, FINE_STEPS, fine_body,
                          (jnp.maximum(center - COARSE_BELOW, KEY_NEG_INF), center + COARSE_ABOVE))
    thr = _key_to_float(lo)

    cnt_gt = count(lambda sc, c: sc > thr)
    cnt_ge = count(lambda sc, c: sc >= thr)
    need = k_sel - cnt_gt
    tie = (cnt_ge - cnt_gt > need) & (thr > -jnp.inf)
    jstar_ref[...] = jnp.full((1, tq), seq, jnp.int32)

    @pl.when(jnp.max(jnp.where(tie, 1.0, 0.0)) > 0.0)
    def _():
        top_bit = seq.bit_length() - 2

        def jbit_body(i, lo):
            trial = lo | lax.shift_left(jnp.int32(1), top_bit - i)
            cnt = count(lambda sc, c: (sc == thr) & (s_iota + c * tk < trial))
            return jnp.where(cnt < need, trial, lo)

        lo = lax.fori_loop(0, top_bit + 1, jbit_body, jnp.zeros((1, tq), jnp.int32))
        jstar_ref[...] = jnp.where(tie, lo, seq)

    jstar = jstar_ref[...]

    def bias_body(c, carry):
        sc = sc_ref[c]
        s_idx = s_iota + c * tk
        sel = (sc > thr) | ((sc == thr) & (s_idx <= jstar))
        bias_ref[c] = jnp.where(sel & (s_idx <= t_idx), 0.0, -jnp.inf)
        return carry

    lax.fori_loop(0, n_chunks, bias_body, 0)

    n_pairs = N_HEADS // 2

    def score_body(c, mx):
        off = pl.multiple_of(c * tk, tk)
        bias = bias_ref[c]
        bias2 = jnp.concatenate([bias, bias], axis=1)
        new = []
        for j in range(n_pairs):
            kc = k_ref[0, pl.ds(off, tk), j * LANES:(j + 1) * LANES]
            s = jnp.dot(kc, qm_ref[j], preferred_element_type=F32) + bias2
            s_ref[j, c] = s
            new.append(jnp.maximum(mx[j], _fold_rows(s, jnp.maximum)))
        return tuple(new)

    mx = lax.fori_loop(0, n_chunks, score_body,
                       (jnp.full((SUBLANES, 2 * tq), -jnp.inf, F32),) * n_pairs)
    m = [jnp.max(mx[j], axis=0, keepdims=True) for j in range(n_pairs)]
    acc_ref[...] = jnp.zeros_like(acc_ref)

    def pv_body(c, l):
        new = []
        for j in range(n_pairs):
            p = jnp.exp2(s_ref[j, c] - m[j])
            vc = vt_ref[0, c, j * LANES:(j + 1) * LANES, :]
            acc_ref[j] += jnp.dot(vc, p.astype(BF16), preferred_element_type=F32)
            new.append(l[j] + _fold_rows(p, jnp.add))
        return tuple(new)

    l = lax.fori_loop(0, n_chunks, pv_body, (jnp.zeros((SUBLANES, 2 * tq), F32),) * n_pairs)
    out_rows = []
    for j in range(n_pairs):
        o = acc_ref[j] / jnp.sum(l[j], axis=0, keepdims=True)
        out_rows.append(o[0:HEAD_DIM, 0:tq])
        out_rows.append(o[HEAD_DIM:LANES, tq:2 * tq])
    o_ref[0] = jnp.concatenate(out_rows, axis=0).T.astype(BF16)


def _attn_call(qt, k, vt, iqt, ik, wt, *, tq, tk):
    bsz, s, _ = k.shape
    k_sel = min(TOPK_MAX, s // 4)
    nc = s // tk
    assert s & (s - 1) == 0 and vt.shape == (bsz, nc, ATTN_WIDTH, tk)
    assert s // BF16_ROWS <= BF16_EXACT_INT
    qfeat = lambda r: pl.BlockSpec((1, r, tq), lambda b, i: (b, 0, i))
    full = lambda w: pl.BlockSpec((1, s, w), lambda b, i: (b, 0, 0))
    return pl.pallas_call(
        functools.partial(_attn_kernel, tq=tq, tk=tk, seq=s, k_sel=k_sel),
        grid=(bsz, s // tq),
        in_specs=[qfeat(ATTN_WIDTH), full(ATTN_WIDTH),
                  pl.BlockSpec((1, nc, ATTN_WIDTH, tk), lambda b, i: (b, 0, 0, 0)),
                  qfeat(ATTN_WIDTH), full(LANES), qfeat(IDX_HEADS)],
        out_specs=pl.BlockSpec((1, tq, ATTN_WIDTH), lambda b, i: (b, i, 0)),
        out_shape=jax.ShapeDtypeStruct((bsz, s, ATTN_WIDTH), BF16),
        scratch_shapes=[pltpu.VMEM((nc, tk, tq), F32),
                        pltpu.VMEM((nc, tk, tq), BF16),
                        pltpu.VMEM((nc, tk, tq), F32),
                        pltpu.VMEM((N_HEADS // 2, nc, tk, 2 * tq), F32),
                        pltpu.VMEM((N_HEADS // 2, LANES, 2 * tq), F32),
                        pltpu.VMEM((N_HEADS // 2, LANES, 2 * tq), BF16),
                        pltpu.VMEM((IDX_HEADS, LANES, tq), BF16),
                        pltpu.VMEM((1, tq), jnp.int32)],
        compiler_params=pltpu.CompilerParams(dimension_semantics=("arbitrary", "arbitrary"),
                                             vmem_limit_bytes=VMEM_LIMIT_BYTES),
        name="attn",
    )(qt, k, vt, iqt, ik, wt)


def _ffn_kernel(x_ref, attn_ref, pool_ref, mod_ref, wo_ref, wg_ref, wu_ref, wd_ref, ln_ref,
                o_ref, acc_ref, *, alpha, n_ff_chunks):
    m = mod_ref[0]
    gate1, shift2, scale2, gate2 = m[2:3, :], m[3:4, :], m[4:5, :], m[5:6, :]
    ln = ln_ref[...]
    mix = jnp.dot(jnp.concatenate([attn_ref[0], pool_ref[0]], axis=1), wo_ref[...],
                  preferred_element_type=F32)
    x1 = _layer_norm(alpha * x_ref[0] + gate1 * mix, ln[0:1, :], ln[1:2, :])
    h2 = (x1 * (1.0 + scale2) + shift2).astype(BF16)
    acc_ref[...] = jnp.zeros_like(acc_ref)

    def ff_body(c, carry):
        g = jnp.dot(h2, wg_ref[c], preferred_element_type=F32)
        u = jnp.dot(h2, wu_ref[c], preferred_element_type=F32)
        a = (g * jax.nn.sigmoid(g) * u).astype(BF16)
        acc_ref[...] += jnp.dot(a, wd_ref[c], preferred_element_type=F32)
        return carry

    lax.fori_loop(0, n_ff_chunks, ff_body, 0, unroll=True)
    o_ref[0] = _layer_norm(alpha * x1 + gate2 * acc_ref[...], ln[2:3, :], ln[3:4, :])


def _ffn_call(x, attn, pool, mod3, wo, wg, wu, wd, ln, *, tm, alpha):
    bsz, s, d = x.shape
    tok = lambda w: pl.BlockSpec((1, tm, w), lambda b, i: (b, i, 0))
    return pl.pallas_call(
        functools.partial(_ffn_kernel, alpha=alpha, n_ff_chunks=wg.shape[0]),
        grid=(bsz, s // tm),
        in_specs=[tok(d), tok(ATTN_WIDTH), tok(POOL_WIDTH),
                  pl.BlockSpec((1, N_MOD, d), lambda b, i: (b, 0, 0)),
                  _const_spec(wo.shape), _const_spec(wg.shape), _const_spec(wu.shape),
                  _const_spec(wd.shape), _const_spec(ln.shape)],
        out_specs=tok(d),
        out_shape=jax.ShapeDtypeStruct((bsz, s, d), F32),
        scratch_shapes=[pltpu.VMEM((tm, d), F32)],
        compiler_params=pltpu.CompilerParams(dimension_semantics=("arbitrary", "arbitrary"),
                                             vmem_limit_bytes=VMEM_LIMIT_BYTES),
        name="ffn",
    )(x, attn, pool, mod3, wo, wg, wu, wd, ln)


def _rope_tables(s):
    inv_freq = ROPE_THETA ** (-jnp.arange(0, HEAD_DIM, 2, dtype=F32) / HEAD_DIM)
    ang = jnp.arange(s, dtype=F32)[:, None] * inv_freq[None, :]
    cos, sin = jnp.cos(ang), jnp.sin(ang)
    reps = LANES // HALF_DIM
    cos_l = jnp.tile(cos, (1, reps))
    sin_l = jnp.tile(jnp.concatenate([-sin, sin], axis=1), (1, reps // 2))
    return jnp.stack([cos_l, sin_l]), jnp.stack([cos.T, sin.T])


def _pick_ff_chunk(d_ff):
    for c in (512, 256, 128):
        if d_ff % c == 0:
            return c
    raise ValueError(f"d_ff={d_ff} is not a multiple of {LANES}")


def kernel(x, c, w_mod, b_mod, w_in, w_pool, pool_scale, w_o, ln1_g, ln1_b, w_gate, w_up, w_down,
           ln2_g, ln2_b):
    bsz, s, d = x.shape
    depth = w_mod.shape[0]
    d_ff = w_gate.shape[-1]
    alpha = (2.0 * depth) ** 0.25
    ts = min(512, s)
    tq = min(256, s)
    tk = min(512, s)
    tm = min(512, s)
    ffc = _pick_ff_chunk(d_ff)
    tab, tabt = _rope_tables(s)
    a = ATTN_WIDTH
    for l in range(depth):
        mod3 = _mod_call(c, w_mod[l], b_mod[l]).reshape(bsz, N_MOD, d)
        wi = w_in[l]
        wq = (wi[:, 0:a] * (HEAD_DIM ** -0.5 * LOG2_E)).astype(BF16)
        wk = wi[:, a:2 * a].astype(BF16)
        wv = wi[:, 2 * a:3 * a].astype(BF16)
        wu = wi[:, 3 * a:3 * a + POOL_WIDTH].astype(BF16)
        o = 3 * a + POOL_WIDTH
        wiq = (wi[:, o:o + IDX_HEADS * IDX_DIM] * IDX_DIM ** -0.5).astype(BF16)
        o += IDX_HEADS * IDX_DIM
        wik = wi[:, o:o + IDX_DIM]
        wiw = wi[:, o + IDX_DIM:o + IDX_DIM + IDX_HEADS]
        wix = jnp.concatenate([wik, wik, wiw, jnp.zeros((d, LANES - IDX_HEADS), F32)], axis=1).astype(BF16)
        qt, k, vt, iqt, ik, wt, pool = _proj_call(
            x, mod3, tab, tabt, wq, wk, wv, wu, wiq, wix, w_pool[l].astype(BF16),
            pool_scale[l].reshape(1, POOL_WIDTH), ts=ts, tk=tk)
        attn = _attn_call(qt, k, vt, iqt, ik, wt, tq=tq, tk=tk)
        wg = w_gate[l].reshape(d, d_ff // ffc, ffc).transpose(1, 0, 2).astype(BF16)
        wup = w_up[l].reshape(d, d_ff // ffc, ffc).transpose(1, 0, 2).astype(BF16)
        wd = w_down[l].reshape(d_ff // ffc, ffc, d).astype(BF16)
        ln = jnp.stack([ln1_g[l], ln1_b[l], ln2_g[l], ln2_b[l]])
        x = _ffn_call(x, attn, pool, mod3, w_o[l].astype(BF16), wg, wup, wd, ln, tm=tm, alpha=alpha)
    return x
```

```python
import functools

import numpy as np
import jax
import jax.numpy as jnp
from jax import lax
from jax.experimental import pallas as pl
from jax.experimental.pallas import tpu as pltpu

N_HEADS = 8
HEAD_DIM = 64
HALF_DIM = HEAD_DIM // 2
ATTN_WIDTH = N_HEADS * HEAD_DIM
IDX_HEADS = 8
IDX_DIM = 64
TOPK_MAX = 256
POOL_WINDOWS = (2, 4, 8, 16)
N_POOL_GROUPS = len(POOL_WINDOWS)
POOL_GROUP_DIM = 128
POOL_WIDTH = N_POOL_GROUPS * POOL_GROUP_DIM
ROPE_THETA = 10000.0
LN_EPS = 1e-5
N_MOD = 6
LOG2_E = 1.4426950408889634

LANES = 128
SUBLANES = 8
POOL_LEVELS = 4
POOL_HALO = SUBLANES * POOL_LEVELS
VMEM_LIMIT_BYTES = 56 * 1024 * 1024

F32 = jnp.float32
BF16 = jnp.bfloat16
INT_MIN = np.int32(-2 ** 31)
KEY_NEG_INF = np.int32(np.int64(0x807FFFFF) - 2 ** 32)


def _const_spec(shape):
    zeros = (0,) * len(shape)
    return pl.BlockSpec(shape, lambda *_: zeros, pipeline_mode=pl.Buffered(1))


def _layer_norm(y, g, b):
    mu = jnp.mean(y, axis=-1, keepdims=True)
    d = y - mu
    var = jnp.mean(d * d, axis=-1, keepdims=True)
    return d * lax.rsqrt(var + LN_EPS) * g + b


def _mod_kernel(c_ref, w_ref, b_ref, o_ref):
    c = c_ref[...]
    a = c * jax.nn.sigmoid(c)
    o_ref[...] = jnp.dot(a, w_ref[...], preferred_element_type=F32,
                         precision=lax.Precision.HIGHEST) + b_ref[...]


def _mod_call(c, w_mod, b_mod):
    bsz, d = c.shape
    n = w_mod.shape[1]
    tn = 1024
    return pl.pallas_call(
        _mod_kernel,
        grid=(n // tn,),
        in_specs=[pl.BlockSpec((bsz, d), lambda j: (0, 0)),
                  pl.BlockSpec((d, tn), lambda j: (0, j)),
                  pl.BlockSpec((1, tn), lambda j: (0, j))],
        out_specs=pl.BlockSpec((bsz, tn), lambda j: (0, j)),
        out_shape=jax.ShapeDtypeStruct((bsz, n), F32),
        compiler_params=pltpu.CompilerParams(dimension_semantics=("arbitrary",),
                                             vmem_limit_bytes=VMEM_LIMIT_BYTES),
        name="mod",
    )(c, w_mod, b_mod.reshape(1, n))


def _proj_kernel(x_ref, mod_ref, tab_ref, tabt_ref, wq_ref, wk_ref, wv_ref, wu_ref, wiq_ref, wix_ref,
                 wpool_ref, ps_ref,
                 qt_ref, k_ref, vt_ref, iqt_ref, ik_ref, wt_ref, pool_ref, ubuf_ref, lvl_ref, *, ts, tk):
    si = pl.program_id(1)
    m = mod_ref[0]
    shift1 = m[0:1, :]
    scale1 = m[1:2, :]
    h = (x_ref[0] * (1.0 + scale1) + shift1).astype(BF16)

    cos = tab_ref[0]
    sin = tab_ref[1]
    lane = lax.broadcasted_iota(jnp.int32, (ts, LANES), 1)
    first_half = (lane & HALF_DIM) == 0

    def rope(y):
        outs = []
        for j in range(y.shape[1] // LANES):
            yj = y[:, j * LANES:(j + 1) * LANES]
            partner = jnp.where(first_half, pltpu.roll(yj, LANES - HALF_DIM, 1),
                                pltpu.roll(yj, HALF_DIM, 1))
            outs.append(yj * cos + partner * sin)
        return outs[0] if len(outs) == 1 else jnp.concatenate(outs, axis=1)

    cos_t = tabt_ref[0]
    sin_t = tabt_ref[1]

    def rope_t(yt):
        outs = []
        for hd in range(yt.shape[0] // HEAD_DIM):
            x1 = yt[hd * HEAD_DIM:hd * HEAD_DIM + HALF_DIM, :]
            x2 = yt[hd * HEAD_DIM + HALF_DIM:(hd + 1) * HEAD_DIM, :]
            outs.append(x1 * cos_t - x2 * sin_t)
            outs.append(x2 * cos_t + x1 * sin_t)
        return jnp.concatenate(outs, axis=0)

    def proj(w_ref):
        return jnp.dot(h, w_ref[...], preferred_element_type=F32)

    qt_ref[0] = rope_t(proj(wq_ref).T).astype(BF16)
    k_ref[0] = rope(proj(wk_ref)).astype(BF16)
    vt = proj(wv_ref).T.astype(BF16)
    for cc in range(ts // tk):
        vt_ref[0, cc] = vt[:, cc * tk:(cc + 1) * tk]
    iqt_ref[0] = rope_t(proj(wiq_ref).T).astype(BF16)
    ix = proj(wix_ref)
    ik_ref[0] = rope(ix[:, :LANES]).astype(BF16)
    wt_ref[0] = (ix[:, LANES:].T)[0:IDX_HEADS, :] * (IDX_HEADS ** -0.5)

    u = proj(wu_ref)

    @pl.when(si == 0)
    def _():
        ubuf_ref[0:POOL_HALO, :] = jnp.zeros((POOL_HALO, POOL_WIDTH), F32)

    ubuf_ref[POOL_HALO:POOL_HALO + ts, :] = u
    n_rows = ts + POOL_HALO
    levels = [ubuf_ref]
    for k in range(1, POOL_LEVELS):
        start, shift = SUBLANES * k, 1 << (k - 1)
        lane0 = next(g for g, win in enumerate(POOL_WINDOWS) if win >= (1 << k)) * POOL_GROUP_DIM
        below = levels[-1]
        lvl_ref[k - 1, start:n_rows, lane0:] = (below[start:n_rows, lane0:]
                                                + below[start - shift:n_rows - shift, lane0:])
        levels.append(lvl_ref.at[k - 1])
    t_pos = si * ts + lax.broadcasted_iota(jnp.int32, (ts, POOL_GROUP_DIM), 0)
    for g, win in enumerate(POOL_WINDOWS):
        lo, hi = g * POOL_GROUP_DIM, (g + 1) * POOL_GROUP_DIM
        ug = u[:, lo:hi]
        e = win.bit_length() - 1
        if e < POOL_LEVELS:
            acc = levels[e][POOL_HALO:n_rows, lo:hi]
        else:
            top, half = levels[POOL_LEVELS - 1], win // 2
            acc = top[POOL_HALO:n_rows, lo:hi] + top[POOL_HALO - half:n_rows - half, lo:hi]
        cnt = jnp.minimum(t_pos + 1, win).astype(F32)
        pooled = acc / cnt - ug
        po = jnp.dot(pooled.astype(BF16), wpool_ref[g], preferred_element_type=F32) * ps_ref[:, lo:hi]
        pool_ref[0, :, lo:hi] = po.astype(BF16)
    ubuf_ref[0:POOL_HALO, :] = ubuf_ref[ts:ts + POOL_HALO, :]


def _proj_call(x, mod3, tab, tabt, wq, wk, wv, wu, wiq, wix, wpool, ps, *, ts, tk):
    bsz, s, d = x.shape
    tok = lambda w: pl.BlockSpec((1, ts, w), lambda b, i: (b, i, 0))
    feat = lambda r: pl.BlockSpec((1, r, ts), lambda b, i: (b, 0, i))
    cpt = ts // tk
    out_shape = [jax.ShapeDtypeStruct((bsz, ATTN_WIDTH, s), BF16),
                 jax.ShapeDtypeStruct((bsz, s, ATTN_WIDTH), BF16),
                 jax.ShapeDtypeStruct((bsz, s // tk, ATTN_WIDTH, tk), BF16),
                 jax.ShapeDtypeStruct((bsz, ATTN_WIDTH, s), BF16),
                 jax.ShapeDtypeStruct((bsz, s, LANES), BF16),
                 jax.ShapeDtypeStruct((bsz, IDX_HEADS, s), F32),
                 jax.ShapeDtypeStruct((bsz, s, POOL_WIDTH), BF16)]
    out_specs = [feat(ATTN_WIDTH), tok(ATTN_WIDTH),
                 pl.BlockSpec((1, cpt, ATTN_WIDTH, tk), lambda b, i: (b, i, 0, 0)),
                 feat(ATTN_WIDTH), tok(LANES), feat(IDX_HEADS), tok(POOL_WIDTH)]
    return pl.pallas_call(
        functools.partial(_proj_kernel, ts=ts, tk=tk),
        grid=(bsz, s // ts),
        in_specs=[tok(d),
                  pl.BlockSpec((1, N_MOD, d), lambda b, i: (b, 0, 0)),
                  pl.BlockSpec((2, ts, LANES), lambda b, i: (0, i, 0)),
                  pl.BlockSpec((2, HALF_DIM, ts), lambda b, i: (0, 0, i)),
                  _const_spec(wq.shape), _const_spec(wk.shape), _const_spec(wv.shape),
                  _const_spec(wu.shape), _const_spec(wiq.shape), _const_spec(wix.shape),
                  _const_spec(wpool.shape), _const_spec(ps.shape)],
        out_specs=out_specs,
        out_shape=out_shape,
        scratch_shapes=[pltpu.VMEM((ts + POOL_HALO, POOL_WIDTH), F32),
                        pltpu.VMEM((POOL_LEVELS - 1, ts + POOL_HALO, POOL_WIDTH), F32)],
        compiler_params=pltpu.CompilerParams(dimension_semantics=("arbitrary", "arbitrary"),
                                             vmem_limit_bytes=VMEM_LIMIT_BYTES),
        name="proj",
    )(x, mod3, tab, tabt, wq, wk, wv, wu, wiq, wix, wpool, ps)


def _key_to_float(key):
    key = jnp.maximum(key, KEY_NEG_INF)
    bits = key ^ ((key >> 31) & np.int32(0x7FFFFFFF))
    return pltpu.bitcast(bits, F32)


def _ukey16_to_float(u):
    u = jnp.maximum(u, UKEY16_NEG_INF)
    bits16 = jnp.where(u >= 0x8000, u ^ 0x8000, u ^ 0xFFFF)
    return pltpu.bitcast(lax.shift_left(bits16, 16), F32)


FOLD_CHAINS = 4
BF16_ROWS = 16
BF16_EXACT_INT = 256
UKEY16_NEG_INF = 0x007F
COARSE_BELOW = 0x8001
COARSE_ABOVE = 0x10001
FINE_STEPS = (COARSE_BELOW + COARSE_ABOVE).bit_length()


def _fold_rows(x, op, rows=SUBLANES):
    blocks = [x[r:r + rows] for r in range(0, x.shape[0], rows)]
    chains = blocks[:FOLD_CHAINS]
    for i, blk in enumerate(blocks[FOLD_CHAINS:]):
        chains[i % FOLD_CHAINS] = op(chains[i % FOLD_CHAINS], blk)
    while len(chains) > 1:
        chains = [op(chains[i], chains[i + 1]) for i in range(0, len(chains) - 1, 2)] + (
            [chains[-1]] if len(chains) % 2 else [])
    return chains[0]


def _attn_kernel(qt_ref, k_ref, vt_ref, iqt_ref, ik_ref, wt_ref, tril_ref, o_ref,
                 sc_ref, scb_ref, bias_ref, s_ref, acc_ref, qm_ref, iqm_ref, *, tq, tk, k_sel):
    qi = pl.program_id(1)
    n_chunks = lax.div(qi * tq + tq + tk - 1, tk)

    zeros_half = jnp.zeros((HEAD_DIM, tq), BF16)
    for h in range(N_HEADS):
        lo = h * HEAD_DIM
        pad = (lambda own: [own, zeros_half]) if h % 2 == 0 else (lambda own: [zeros_half, own])
        iqm_ref[h] = jnp.concatenate(pad(iqt_ref[0, lo:lo + HEAD_DIM, :]), axis=0)
        qm_ref[h // 2, :, (h % 2) * tq:(h % 2 + 1) * tq] = jnp.concatenate(
            pad(qt_ref[0, lo:lo + HEAD_DIM, :]), axis=0)

    t_idx = qi * tq + lax.broadcasted_iota(jnp.int32, (tk, tq), 1)
    s_iota = lax.broadcasted_iota(jnp.int32, (tk, tq), 0)

    def idx_body(c, carry):
        off = pl.multiple_of(c * tk, tk)
        ikc = ik_ref[0, pl.ds(off, tk), :]
        acc = jnp.zeros((tk, tq), F32)
        for h in range(IDX_HEADS):
            lg = jnp.dot(ikc, iqm_ref[h], preferred_element_type=F32)
            acc = acc + wt_ref[0, h:h + 1, :] * jnp.maximum(lg, 0.0)
        sc = jnp.where(s_iota + off <= t_idx, acc, -jnp.inf)
        sc_ref[c] = sc
        scb_ref[c] = sc.astype(BF16)
        return carry

    lax.fori_loop(0, n_chunks, idx_body, 0)

    def count(pred):
        def body(c, cnt):
            return cnt + _fold_rows(jnp.where(pred(sc_ref[c], c), 1.0, 0.0), jnp.add)
        cnt = lax.fori_loop(0, n_chunks, body, jnp.zeros((SUBLANES, tq), F32))
        return jnp.sum(cnt, axis=0, keepdims=True)

    def count_coarse(trial_b):
        one, zero = jnp.ones((), BF16), jnp.zeros((), BF16)

        def body(c, cnt):
            return cnt + _fold_rows(jnp.where(scb_ref[c] >= trial_b, one, zero), jnp.add, BF16_ROWS)
        cnt = lax.fori_loop(0, n_chunks, body, jnp.zeros((BF16_ROWS, tq), BF16))
        return jnp.sum(cnt.astype(F32), axis=0, keepdims=True)

    def coarse_body(i, cand):
        trial = cand | lax.shift_left(jnp.int32(1), 15 - i)
        cnt = count_coarse(_ukey16_to_float(trial).astype(BF16))
        return jnp.where(cnt >= k_sel, trial, cand)

    cand16 = lax.fori_loop(0, 16, coarse_body, jnp.zeros((1, tq), jnp.int32))
    cand16 = jnp.maximum(cand16, UKEY16_NEG_INF)
    center = (lax.shift_left(cand16, 16) | jnp.where(cand16 >= 0x8000, 0, 0xFFFF)) ^ INT_MIN

    def fine_body(i, carry):
        lo, hi = carry
        mid = lo + lax.shift_right_logical(hi - lo, 1)
        feasible = count(lambda sc, c: sc >= _key_to_float(mid)) >= k_sel
        return jnp.where(feasible, mid, lo), jnp.where(feasible, hi, mid)

    lo, _ = lax.fori_loop(0, FINE_STEPS, fine_body,
                          (jnp.maximum(center - COARSE_BELOW, KEY_NEG_INF), center + COARSE_ABOVE))
    thr = _key_to_float(lo)

    cnt_gt = count(lambda sc, c: sc > thr)
    need = jnp.where(thr > -jnp.inf, k_sel - cnt_gt, 0.0)

    def bias_body(c, before):
        sc = sc_ref[c]
        tied = sc == thr
        rank = before + jnp.dot(tril_ref[...], jnp.where(tied, 1.0, 0.0).astype(BF16),
                                preferred_element_type=F32)
        keep = jnp.where(sc > thr, 0.0, jnp.where(tied, jnp.where(rank <= need, 0.0, -jnp.inf), -jnp.inf))
        bias_ref[c] = keep
        return rank[tk - 1:tk, :]

    lax.fori_loop(0, n_chunks, bias_body, jnp.zeros((1, tq), F32))

    n_pairs = N_HEADS // 2

    def score_body(c, mx):
        off = pl.multiple_of(c * tk, tk)
        bias = bias_ref[c]
        bias2 = jnp.concatenate([bias, bias], axis=1)
        new = []
        for j in range(n_pairs):
            kc = k_ref[0, pl.ds(off, tk), j * LANES:(j + 1) * LANES]
            s = jnp.dot(kc, qm_ref[j], preferred_element_type=F32) + bias2
            s_ref[j, c] = s
            new.append(jnp.maximum(mx[j], _fold_rows(s, jnp.maximum)))
        return tuple(new)

    mx = lax.fori_loop(0, n_chunks, score_body,
                       (jnp.full((SUBLANES, 2 * tq), -jnp.inf, F32),) * n_pairs)
    m = [jnp.max(mx[j], axis=0, keepdims=True) for j in range(n_pairs)]
    acc_ref[...] = jnp.zeros_like(acc_ref)

    def pv_body(c, l):
        new = []
        for j in range(n_pairs):
            p = jnp.exp2(s_ref[j, c] - m[j])
            vc = vt_ref[0, c, j * LANES:(j + 1) * LANES, :]
            acc_ref[j] += jnp.dot(vc, p.astype(BF16), preferred_element_type=F32)
            new.append(l[j] + _fold_rows(p, jnp.add))
        return tuple(new)

    l = lax.fori_loop(0, n_chunks, pv_body, (jnp.zeros((SUBLANES, 2 * tq), F32),) * n_pairs)
    out_rows = []
    for j in range(n_pairs):
        o = acc_ref[j] / jnp.sum(l[j], axis=0, keepdims=True)
        out_rows.append(o[0:HEAD_DIM, 0:tq])
        out_rows.append(o[HEAD_DIM:LANES, tq:2 * tq])
    o_ref[0] = jnp.concatenate(out_rows, axis=0).T.astype(BF16)


def _attn_call(qt, k, vt, iqt, ik, wt, *, tq, tk):
    bsz, s, _ = k.shape
    k_sel = min(TOPK_MAX, s // 4)
    nc = s // tk
    assert vt.shape == (bsz, nc, ATTN_WIDTH, tk)
    assert s // BF16_ROWS <= BF16_EXACT_INT
    qfeat = lambda r: pl.BlockSpec((1, r, tq), lambda b, i: (b, 0, i))
    full = lambda w: pl.BlockSpec((1, s, w), lambda b, i: (b, 0, 0))
    tril = jnp.tril(jnp.ones((tk, tk), BF16))
    return pl.pallas_call(
        functools.partial(_attn_kernel, tq=tq, tk=tk, k_sel=k_sel),
        grid=(bsz, s // tq),
        in_specs=[qfeat(ATTN_WIDTH), full(ATTN_WIDTH),
                  pl.BlockSpec((1, nc, ATTN_WIDTH, tk), lambda b, i: (b, 0, 0, 0)),
                  qfeat(ATTN_WIDTH), full(LANES), qfeat(IDX_HEADS), _const_spec((tk, tk))],
        out_specs=pl.BlockSpec((1, tq, ATTN_WIDTH), lambda b, i: (b, i, 0)),
        out_shape=jax.ShapeDtypeStruct((bsz, s, ATTN_WIDTH), BF16),
        scratch_shapes=[pltpu.VMEM((nc, tk, tq), F32),
                        pltpu.VMEM((nc, tk, tq), BF16),
                        pltpu.VMEM((nc, tk, tq), F32),
                        pltpu.VMEM((N_HEADS // 2, nc, tk, 2 * tq), F32),
                        pltpu.VMEM((N_HEADS // 2, LANES, 2 * tq), F32),
                        pltpu.VMEM((N_HEADS // 2, LANES, 2 * tq), BF16),
                        pltpu.VMEM((IDX_HEADS, LANES, tq), BF16)],
        compiler_params=pltpu.CompilerParams(dimension_semantics=("arbitrary", "arbitrary"),
                                             vmem_limit_bytes=VMEM_LIMIT_BYTES),
        name="attn",
    )(qt, k, vt, iqt, ik, wt, tril)


def _ffn_kernel(x_ref, attn_ref, pool_ref, mod_ref, wo_ref, wg_ref, wu_ref, wd_ref, ln_ref,
                o_ref, acc_ref, *, alpha, ffc):
    m = mod_ref[0]
    gate1, shift2, scale2, gate2 = m[2:3, :], m[3:4, :], m[4:5, :], m[5:6, :]
    ln = ln_ref[...]
    mix = jnp.dot(jnp.concatenate([attn_ref[0], pool_ref[0]], axis=1), wo_ref[...],
                  preferred_element_type=F32)
    x1 = _layer_norm(alpha * x_ref[0] + gate1 * mix, ln[0:1, :], ln[1:2, :])
    h2 = (x1 * (1.0 + scale2) + shift2).astype(BF16)
    acc_ref[...] = jnp.zeros_like(acc_ref)

    for c0 in range(0, wg_ref.shape[1], ffc):
        g = jnp.dot(h2, wg_ref[:, c0:c0 + ffc], preferred_element_type=F32)
        u = jnp.dot(h2, wu_ref[:, c0:c0 + ffc], preferred_element_type=F32)
        a = (g * jax.nn.sigmoid(g) * u).astype(BF16)
        acc_ref[...] += jnp.dot(a, wd_ref[c0:c0 + ffc, :], preferred_element_type=F32)
    o_ref[0] = _layer_norm(alpha * x1 + gate2 * acc_ref[...], ln[2:3, :], ln[3:4, :])


def _ffn_call(x, attn, pool, mod3, wo, wg, wu, wd, ln, *, tm, alpha, ffc):
    bsz, s, d = x.shape
    tok = lambda w: pl.BlockSpec((1, tm, w), lambda b, i: (b, i, 0))
    return pl.pallas_call(
        functools.partial(_ffn_kernel, alpha=alpha, ffc=ffc),
        grid=(bsz, s // tm),
        in_specs=[tok(d), tok(ATTN_WIDTH), tok(POOL_WIDTH),
                  pl.BlockSpec((1, N_MOD, d), lambda b, i: (b, 0, 0)),
                  _const_spec(wo.shape), _const_spec(wg.shape), _const_spec(wu.shape),
                  _const_spec(wd.shape), _const_spec(ln.shape)],
        out_specs=tok(d),
        out_shape=jax.ShapeDtypeStruct((bsz, s, d), F32),
        scratch_shapes=[pltpu.VMEM((tm, d), F32)],
        compiler_params=pltpu.CompilerParams(dimension_semantics=("arbitrary", "arbitrary"),
                                             vmem_limit_bytes=VMEM_LIMIT_BYTES),
        name="ffn",
    )(x, attn, pool, mod3, wo, wg, wu, wd, ln)


def _rope_tables(s):
    inv_freq = ROPE_THETA ** (-jnp.arange(0, HEAD_DIM, 2, dtype=F32) / HEAD_DIM)
    ang = jnp.arange(s, dtype=F32)[:, None] * inv_freq[None, :]
    cos, sin = jnp.cos(ang), jnp.sin(ang)
    reps = LANES // HALF_DIM
    cos_l = jnp.tile(cos, (1, reps))
    sin_l = jnp.tile(jnp.concatenate([-sin, sin], axis=1), (1, reps // 2))
    return jnp.stack([cos_l, sin_l]), jnp.stack([cos.T, sin.T])


def _pick_ff_chunk(d_ff):
    for c in (512, 256, 128):
        if d_ff % c == 0:
            return c
    raise ValueError(f"d_ff={d_ff} is not a multiple of {LANES}")


def kernel(x, c, w_mod, b_mod, w_in, w_pool, pool_scale, w_o, ln1_g, ln1_b, w_gate, w_up, w_down,
           ln2_g, ln2_b):
    bsz, s, d = x.shape
    depth = w_mod.shape[0]
    d_ff = w_gate.shape[-1]
    alpha = (2.0 * depth) ** 0.25
    ts = min(512, s)
    tq = min(256, s)
    tk = min(512, s)
    tm = min(512, s)
    ffc = _pick_ff_chunk(d_ff)
    tab, tabt = _rope_tables(s)
    a = ATTN_WIDTH
    for l in range(depth):
        mod3 = _mod_call(c, w_mod[l], b_mod[l]).reshape(bsz, N_MOD, d)
        wi = w_in[l]
        wq = (wi[:, 0:a] * (HEAD_DIM ** -0.5 * LOG2_E)).astype(BF16)
        wk = wi[:, a:2 * a].astype(BF16)
        wv = wi[:, 2 * a:3 * a].astype(BF16)
        wu = wi[:, 3 * a:3 * a + POOL_WIDTH].astype(BF16)
        o = 3 * a + POOL_WIDTH
        wiq = (wi[:, o:o + IDX_HEADS * IDX_DIM] * IDX_DIM ** -0.5).astype(BF16)
        o += IDX_HEADS * IDX_DIM
        wik = wi[:, o:o + IDX_DIM]
        wiw = wi[:, o + IDX_DIM:o + IDX_DIM + IDX_HEADS]
        wix = jnp.concatenate([wik, wik, wiw, jnp.zeros((d, LANES - IDX_HEADS), F32)], axis=1).astype(BF16)
        qt, k, vt, iqt, ik, wt, pool = _proj_call(
            x, mod3, tab, tabt, wq, wk, wv, wu, wiq, wix, w_pool[l].astype(BF16),
            pool_scale[l].reshape(1, POOL_WIDTH), ts=ts, tk=tk)
        attn = _attn_call(qt, k, vt, iqt, ik, wt, tq=tq, tk=tk)
        ln = jnp.stack([ln1_g[l], ln1_b[l], ln2_g[l], ln2_b[l]])
        x = _ffn_call(x, attn, pool, mod3, w_o[l].astype(BF16), w_gate[l].astype(BF16),
                      w_up[l].astype(BF16), w_down[l].astype(BF16), ln, tm=tm, alpha=alpha, ffc=ffc)
    return x
```

```python
import functools

import numpy as np
import jax
import jax.numpy as jnp
from jax import lax
from jax.experimental import pallas as pl
from jax.experimental.pallas import tpu as pltpu

N_HEADS = 8
HEAD_DIM = 64
HALF_DIM = HEAD_DIM // 2
ATTN_WIDTH = N_HEADS * HEAD_DIM
IDX_HEADS = 8
IDX_DIM = 64
TOPK_MAX = 256
POOL_WINDOWS = (2, 4, 8, 16)
N_POOL_GROUPS = len(POOL_WINDOWS)
POOL_GROUP_DIM = 128
POOL_WIDTH = N_POOL_GROUPS * POOL_GROUP_DIM
ROPE_THETA = 10000.0
LN_EPS = 1e-5
N_MOD = 6
LOG2_E = 1.4426950408889634

LANES = 128
SUBLANES = 8
POOL_LEVELS = 4
POOL_HALO = SUBLANES * POOL_LEVELS
VMEM_LIMIT_BYTES = 56 * 1024 * 1024

F32 = jnp.float32
BF16 = jnp.bfloat16
INT_MIN = np.int32(-2 ** 31)
KEY_NEG_INF = np.int32(np.int64(0x807FFFFF) - 2 ** 32)


def _const_spec(shape):
    zeros = (0,) * len(shape)
    return pl.BlockSpec(shape, lambda *_: zeros, pipeline_mode=pl.Buffered(1))


def _layer_norm(y, g, b):
    mu = jnp.mean(y, axis=-1, keepdims=True)
    d = y - mu
    var = jnp.mean(d * d, axis=-1, keepdims=True)
    return d * lax.rsqrt(var + LN_EPS) * g + b


def _mod_kernel(c_ref, w_ref, b_ref, o_ref):
    c = c_ref[...]
    a = c * jax.nn.sigmoid(c)
    o_ref[...] = jnp.dot(a, w_ref[...], preferred_element_type=F32,
                         precision=lax.Precision.HIGHEST) + b_ref[...]


def _mod_call(c, w_mod, b_mod):
    bsz, d = c.shape
    n = w_mod.shape[1]
    tn = 1024
    return pl.pallas_call(
        _mod_kernel,
        grid=(n // tn,),
        in_specs=[pl.BlockSpec((bsz, d), lambda j: (0, 0)),
                  pl.BlockSpec((d, tn), lambda j: (0, j)),
                  pl.BlockSpec((1, tn), lambda j: (0, j))],
        out_specs=pl.BlockSpec((bsz, tn), lambda j: (0, j)),
        out_shape=jax.ShapeDtypeStruct((bsz, n), F32),
        compiler_params=pltpu.CompilerParams(dimension_semantics=("arbitrary",),
                                             vmem_limit_bytes=VMEM_LIMIT_BYTES),
        name="mod",
    )(c, w_mod, b_mod.reshape(1, n))


def _proj_kernel(x_ref, mod_ref, tab_ref, tabt_ref, wq_ref, wk_ref, wv_ref, wu_ref, wiq_ref, wix_ref,
                 wpool_ref, ps_ref,
                 qt_ref, k_ref, vt_ref, iqt_ref, ik_ref, wt_ref, pool_ref, ubuf_ref, lvl_ref, *, ts, tk):
    si = pl.program_id(1)

    @pl.when(si == 0)
    def _():
        ubuf_ref[0:POOL_HALO, :] = jnp.zeros((POOL_HALO, POOL_WIDTH), F32)

    m = mod_ref[0]
    shift1 = m[0:1, :]
    scale1 = m[1:2, :]
    h = (x_ref[0] * (1.0 + scale1) + shift1).astype(BF16)

    cos = tab_ref[0]
    sin = tab_ref[1]
    lane = lax.broadcasted_iota(jnp.int32, (ts, LANES), 1)
    first_half = (lane & HALF_DIM) == 0

    def rope(y):
        outs = []
        for j in range(y.shape[1] // LANES):
            yj = y[:, j * LANES:(j + 1) * LANES]
            partner = jnp.where(first_half, pltpu.roll(yj, LANES - HALF_DIM, 1),
                                pltpu.roll(yj, HALF_DIM, 1))
            outs.append(yj * cos + partner * sin)
        return outs[0] if len(outs) == 1 else jnp.concatenate(outs, axis=1)

    cos_t = tabt_ref[0]
    sin_t = tabt_ref[1]

    def rope_t(yt):
        outs = []
        for hd in range(yt.shape[0] // HEAD_DIM):
            x1 = yt[hd * HEAD_DIM:hd * HEAD_DIM + HALF_DIM, :]
            x2 = yt[hd * HEAD_DIM + HALF_DIM:(hd + 1) * HEAD_DIM, :]
            outs.append(x1 * cos_t - x2 * sin_t)
            outs.append(x2 * cos_t + x1 * sin_t)
        return jnp.concatenate(outs, axis=0)

    def proj(w_ref):
        return jnp.dot(h, w_ref[...], preferred_element_type=F32)

    qt_ref[0] = rope_t(proj(wq_ref).T).astype(BF16)
    k_ref[0] = rope(proj(wk_ref)).astype(BF16)
    vt = proj(wv_ref).T.astype(BF16)
    for cc in range(ts // tk):
        vt_ref[0, cc] = vt[:, cc * tk:(cc + 1) * tk]
    iqt_ref[0] = rope_t(proj(wiq_ref).T).astype(BF16)
    ix = proj(wix_ref)
    ik_ref[0] = rope(ix[:, :LANES]).astype(BF16)
    wt_ref[0] = (ix[:, LANES:].T)[0:IDX_HEADS, :] * (IDX_HEADS ** -0.5)

    u = proj(wu_ref)
    ubuf_ref[POOL_HALO:POOL_HALO + ts, :] = u
    n_rows = ts + POOL_HALO
    levels = [ubuf_ref]
    for k in range(1, POOL_LEVELS):
        start, shift = SUBLANES * k, 1 << (k - 1)
        lane0 = next(g for g, win in enumerate(POOL_WINDOWS) if win >= (1 << k)) * POOL_GROUP_DIM
        below = levels[-1]
        lvl_ref[k - 1, start:n_rows, lane0:] = (below[start:n_rows, lane0:]
                                                + below[start - shift:n_rows - shift, lane0:])
        levels.append(lvl_ref.at[k - 1])
    t_pos = si * ts + lax.broadcasted_iota(jnp.int32, (ts, POOL_GROUP_DIM), 0)
    for g, win in enumerate(POOL_WINDOWS):
        lo, hi = g * POOL_GROUP_DIM, (g + 1) * POOL_GROUP_DIM
        ug = u[:, lo:hi]
        e = win.bit_length() - 1
        if e < POOL_LEVELS:
            acc = levels[e][POOL_HALO:n_rows, lo:hi]
        else:
            top, half = levels[POOL_LEVELS - 1], win // 2
            acc = top[POOL_HALO:n_rows, lo:hi] + top[POOL_HALO - half:n_rows - half, lo:hi]
        cnt = jnp.minimum(t_pos + 1, win).astype(F32)
        pooled = acc / cnt - ug
        po = jnp.dot(pooled.astype(BF16), wpool_ref[g], preferred_element_type=F32) * ps_ref[:, lo:hi]
        pool_ref[0, :, lo:hi] = po.astype(BF16)
    ubuf_ref[0:POOL_HALO, :] = ubuf_ref[ts:ts + POOL_HALO, :]


def _proj_call(x, mod3, tab, tabt, wq, wk, wv, wu, wiq, wix, wpool, ps, *, ts, tk):
    bsz, s, d = x.shape
    tok = lambda w: pl.BlockSpec((1, ts, w), lambda b, i: (b, i, 0))
    feat = lambda r: pl.BlockSpec((1, r, ts), lambda b, i: (b, 0, i))
    cpt = ts // tk
    out_shape = [jax.ShapeDtypeStruct((bsz, ATTN_WIDTH, s), BF16),
                 jax.ShapeDtypeStruct((bsz, s, ATTN_WIDTH), BF16),
                 jax.ShapeDtypeStruct((bsz, s // tk, ATTN_WIDTH, tk), BF16),
                 jax.ShapeDtypeStruct((bsz, ATTN_WIDTH, s), BF16),
                 jax.ShapeDtypeStruct((bsz, s, LANES), BF16),
                 jax.ShapeDtypeStruct((bsz, IDX_HEADS, s), F32),
                 jax.ShapeDtypeStruct((bsz, s, POOL_WIDTH), BF16)]
    out_specs = [feat(ATTN_WIDTH), tok(ATTN_WIDTH),
                 pl.BlockSpec((1, cpt, ATTN_WIDTH, tk), lambda b, i: (b, i, 0, 0)),
                 feat(ATTN_WIDTH), tok(LANES), feat(IDX_HEADS), tok(POOL_WIDTH)]
    return pl.pallas_call(
        functools.partial(_proj_kernel, ts=ts, tk=tk),
        grid=(bsz, s // ts),
        in_specs=[tok(d),
                  pl.BlockSpec((1, N_MOD, d), lambda b, i: (b, 0, 0)),
                  pl.BlockSpec((2, ts, LANES), lambda b, i: (0, i, 0)),
                  pl.BlockSpec((2, HALF_DIM, ts), lambda b, i: (0, 0, i)),
                  _const_spec(wq.shape), _const_spec(wk.shape), _const_spec(wv.shape),
                  _const_spec(wu.shape), _const_spec(wiq.shape), _const_spec(wix.shape),
                  _const_spec(wpool.shape), _const_spec(ps.shape)],
        out_specs=out_specs,
        out_shape=out_shape,
        scratch_shapes=[pltpu.VMEM((ts + POOL_HALO, POOL_WIDTH), F32),
                        pltpu.VMEM((POOL_LEVELS - 1, ts + POOL_HALO, POOL_WIDTH), F32)],
        compiler_params=pltpu.CompilerParams(dimension_semantics=("arbitrary", "arbitrary"),
                                             vmem_limit_bytes=VMEM_LIMIT_BYTES),
        name="proj",
    )(x, mod3, tab, tabt, wq, wk, wv, wu, wiq, wix, wpool, ps)


def _key_to_float(key):
    key = jnp.maximum(key, KEY_NEG_INF)
    bits = key ^ ((key >> 31) & np.int32(0x7FFFFFFF))
    return pltpu.bitcast(bits, F32)


def _ukey16_to_float(u):
    u = jnp.maximum(u, UKEY16_NEG_INF)
    bits16 = jnp.where(u >= 0x8000, u ^ 0x8000, u ^ 0xFFFF)
    return pltpu.bitcast(lax.shift_left(bits16, 16), F32)


FOLD_CHAINS = 4
BF16_ROWS = 16
BF16_EXACT_INT = 256
UKEY16_NEG_INF = 0x007F
COARSE_BELOW = 0x8001
COARSE_ABOVE = 0x10001
FINE_STEPS = (COARSE_BELOW + COARSE_ABOVE).bit_length()


def _fold_rows(x, op, rows=SUBLANES):
    blocks = [x[r:r + rows] for r in range(0, x.shape[0], rows)]
    chains = blocks[:FOLD_CHAINS]
    for i, blk in enumerate(blocks[FOLD_CHAINS:]):
        chains[i % FOLD_CHAINS] = op(chains[i % FOLD_CHAINS], blk)
    while len(chains) > 1:
        chains = [op(chains[i], chains[i + 1]) for i in range(0, len(chains) - 1, 2)] + (
            [chains[-1]] if len(chains) % 2 else [])
    return chains[0]


def _attn_kernel(qt_ref, k_ref, vt_ref, iqt_ref, ik_ref, wt_ref, tril_ref, o_ref,
                 sc_ref, scb_ref, bias_ref, s_ref, acc_ref, qm_ref, iqm_ref, *, tq, tk, k_sel):
    qi = pl.program_id(1)
    n_chunks = lax.div(qi * tq + tq + tk - 1, tk)

    zeros_half = jnp.zeros((HEAD_DIM, tq), BF16)
    for h in range(N_HEADS):
        lo = h * HEAD_DIM
        pad = (lambda own: [own, zeros_half]) if h % 2 == 0 else (lambda own: [zeros_half, own])
        iqm_ref[h] = jnp.concatenate(pad(iqt_ref[0, lo:lo + HEAD_DIM, :]), axis=0)
        qm_ref[h // 2, :, (h % 2) * tq:(h % 2 + 1) * tq] = jnp.concatenate(
            pad(qt_ref[0, lo:lo + HEAD_DIM, :]), axis=0)

    def chunk_pairs(body, init):
        carry = lax.fori_loop(0, lax.shift_right_logical(n_chunks, 1),
                              lambda i, cr: body(2 * i + 1, body(2 * i, cr)), init)
        return lax.cond((n_chunks & 1) == 1, lambda cr: body(n_chunks - 1, cr), lambda cr: cr, carry)

    t_idx = qi * tq + lax.broadcasted_iota(jnp.int32, (tk, tq), 1)
    s_iota = lax.broadcasted_iota(jnp.int32, (tk, tq), 0)

    def idx_body(c, carry):
        off = pl.multiple_of(c * tk, tk)
        ikc = ik_ref[0, pl.ds(off, tk), :]
        acc = jnp.zeros((tk, tq), F32)
        for h in range(IDX_HEADS):
            lg = jnp.dot(ikc, iqm_ref[h], preferred_element_type=F32)
            acc = acc + wt_ref[0, h:h + 1, :] * jnp.maximum(lg, 0.0)
        sc = jnp.where(s_iota + off <= t_idx, acc, -jnp.inf)
        sc_ref[c] = sc
        scb_ref[c] = sc.astype(BF16)
        return carry

    chunk_pairs(idx_body, 0)

    def count(pred):
        def body(c, cnt):
            return cnt + _fold_rows(jnp.where(pred(sc_ref[c], c), 1.0, 0.0), jnp.add)
        cnt = lax.fori_loop(0, n_chunks, body, jnp.zeros((SUBLANES, tq), F32))
        return jnp.sum(cnt, axis=0, keepdims=True)

    def count_coarse(trial_b):
        one, zero = jnp.ones((), BF16), jnp.zeros((), BF16)

        def body(c, cnt):
            return cnt + _fold_rows(jnp.where(scb_ref[c] >= trial_b, one, zero), jnp.add, BF16_ROWS)
        cnt = lax.fori_loop(0, n_chunks, body, jnp.zeros((BF16_ROWS, tq), BF16))
        return jnp.sum(cnt.astype(F32), axis=0, keepdims=True)

    def coarse_body(i, cand):
        trial = cand | lax.shift_left(jnp.int32(1), 15 - i)
        cnt = count_coarse(_ukey16_to_float(trial).astype(BF16))
        return jnp.where(cnt >= k_sel, trial, cand)

    cand16 = lax.fori_loop(0, 16, coarse_body, jnp.zeros((1, tq), jnp.int32))
    cand16 = jnp.maximum(cand16, UKEY16_NEG_INF)
    center = (lax.shift_left(cand16, 16) | jnp.where(cand16 >= 0x8000, 0, 0xFFFF)) ^ INT_MIN

    def fine_body(i, carry):
        lo, hi, cnt_hi = carry
        mid = lo + lax.shift_right_logical(hi - lo, 1)
        cnt = count(lambda sc, c: sc >= _key_to_float(mid))
        feasible = cnt >= k_sel
        return jnp.where(feasible, mid, lo), jnp.where(feasible, hi, mid), jnp.where(feasible, cnt_hi, cnt)

    lo, _, cnt_hi = lax.fori_loop(
        0, FINE_STEPS, fine_body,
        (jnp.maximum(center - COARSE_BELOW, KEY_NEG_INF), center + COARSE_ABOVE, jnp.full((1, tq), -1.0, F32)))
    thr = _key_to_float(lo)

    cnt_gt = lax.cond(jnp.min(cnt_hi) < 0.0, lambda: count(lambda sc, c: sc > thr), lambda: cnt_hi)
    need = jnp.where(thr > -jnp.inf, k_sel - cnt_gt, 0.0)

    def bias_body(c, before):
        sc = sc_ref[c]
        tied = sc == thr
        rank = before + jnp.dot(tril_ref[...], jnp.where(tied, 1.0, 0.0).astype(BF16),
                                preferred_element_type=F32)
        keep = jnp.where(sc > thr, 0.0, jnp.where(tied, jnp.where(rank <= need, 0.0, -jnp.inf), -jnp.inf))
        bias_ref[c] = keep
        return rank[tk - 1:tk, :]

    lax.fori_loop(0, n_chunks, bias_body, jnp.zeros((1, tq), F32))

    n_pairs = N_HEADS // 2

    def score_body(c, mx):
        off = pl.multiple_of(c * tk, tk)
        bias = bias_ref[c]
        bias2 = jnp.concatenate([bias, bias], axis=1)
        new = []
        for j in range(n_pairs):
            kc = k_ref[0, pl.ds(off, tk), j * LANES:(j + 1) * LANES]
            s = jnp.dot(kc, qm_ref[j], preferred_element_type=F32) + bias2
            s_ref[j, c] = s
            new.append(jnp.maximum(mx[j], _fold_rows(s, jnp.maximum)))
        return tuple(new)

    mx = chunk_pairs(score_body, (jnp.full((SUBLANES, 2 * tq), -jnp.inf, F32),) * n_pairs)
    m = [jnp.max(mx[j], axis=0, keepdims=True) for j in range(n_pairs)]
    acc_ref[...] = jnp.zeros_like(acc_ref)

    def pv_body(c, l):
        new = []
        for j in range(n_pairs):
            p = jnp.exp2(s_ref[j, c] - m[j])
            vc = vt_ref[0, c, j * LANES:(j + 1) * LANES, :]
            acc_ref[j] += jnp.dot(vc, p.astype(BF16), preferred_element_type=F32)
            new.append(l[j] + _fold_rows(p, jnp.add))
        return tuple(new)

    l = chunk_pairs(pv_body, (jnp.zeros((SUBLANES, 2 * tq), F32),) * n_pairs)
    out_rows = []
    for j in range(n_pairs):
        o = acc_ref[j] / jnp.sum(l[j], axis=0, keepdims=True)
        out_rows.append(o[0:HEAD_DIM, 0:tq])
        out_rows.append(o[HEAD_DIM:LANES, tq:2 * tq])
    o_ref[0] = jnp.concatenate(out_rows, axis=0).T.astype(BF16)


def _attn_call(qt, k, vt, iqt, ik, wt, *, tq, tk):
    bsz, s, _ = k.shape
    k_sel = min(TOPK_MAX, s // 4)
    nc = s // tk
    assert vt.shape == (bsz, nc, ATTN_WIDTH, tk)
    assert s // BF16_ROWS <= BF16_EXACT_INT
    qfeat = lambda r: pl.BlockSpec((1, r, tq), lambda b, i: (b, 0, i))
    full = lambda w: pl.BlockSpec((1, s, w), lambda b, i: (b, 0, 0))
    tril = jnp.tril(jnp.ones((tk, tk), BF16))
    return pl.pallas_call(
        functools.partial(_attn_kernel, tq=tq, tk=tk, k_sel=k_sel),
        grid=(bsz, s // tq),
        in_specs=[qfeat(ATTN_WIDTH), full(ATTN_WIDTH),
                  pl.BlockSpec((1, nc, ATTN_WIDTH, tk), lambda b, i: (b, 0, 0, 0)),
                  qfeat(ATTN_WIDTH), full(LANES), qfeat(IDX_HEADS), _const_spec((tk, tk))],
        out_specs=pl.BlockSpec((1, tq, ATTN_WIDTH), lambda b, i: (b, i, 0)),
        out_shape=jax.ShapeDtypeStruct((bsz, s, ATTN_WIDTH), BF16),
        scratch_shapes=[pltpu.VMEM((nc, tk, tq), F32),
                        pltpu.VMEM((nc, tk, tq), BF16),
                        pltpu.VMEM((nc, tk, tq), F32),
                        pltpu.VMEM((N_HEADS // 2, nc, tk, 2 * tq), F32),
                        pltpu.VMEM((N_HEADS // 2, LANES, 2 * tq), F32),
                        pltpu.VMEM((N_HEADS // 2, LANES, 2 * tq), BF16),
                        pltpu.VMEM((IDX_HEADS, LANES, tq), BF16)],
        compiler_params=pltpu.CompilerParams(dimension_semantics=("arbitrary", "arbitrary"),
                                             vmem_limit_bytes=VMEM_LIMIT_BYTES),
        name="attn",
    )(qt, k, vt, iqt, ik, wt, tril)


def _ffn_kernel(x_ref, attn_ref, pool_ref, mod_ref, wo_ref, wg_ref, wu_ref, wd_ref, ln_ref,
                o_ref, acc_ref, *, alpha, ffc):
    m = mod_ref[0]
    gate1, shift2, scale2, gate2 = m[2:3, :], m[3:4, :], m[4:5, :], m[5:6, :]
    ln = ln_ref[...]
    mix = jnp.dot(jnp.concatenate([attn_ref[0], pool_ref[0]], axis=1), wo_ref[...],
                  preferred_element_type=F32)
    x1 = _layer_norm(alpha * x_ref[0] + gate1 * mix, ln[0:1, :], ln[1:2, :])
    h2 = (x1 * (1.0 + scale2) + shift2).astype(BF16)
    acc_ref[...] = jnp.zeros_like(acc_ref)

    for c0 in range(0, wg_ref.shape[1], ffc):
        g = jnp.dot(h2, wg_ref[:, c0:c0 + ffc], preferred_element_type=F32)
        u = jnp.dot(h2, wu_ref[:, c0:c0 + ffc], preferred_element_type=F32)
        a = (g * jax.nn.sigmoid(g) * u).astype(BF16)
        acc_ref[...] += jnp.dot(a, wd_ref[c0:c0 + ffc, :], preferred_element_type=F32)
    o_ref[0] = _layer_norm(alpha * x1 + gate2 * acc_ref[...], ln[2:3, :], ln[3:4, :])


def _ffn_call(x, attn, pool, mod3, wo, wg, wu, wd, ln, *, tm, alpha, ffc):
    bsz, s, d = x.shape
    tok = lambda w: pl.BlockSpec((1, tm, w), lambda b, i: (b, i, 0))
    return pl.pallas_call(
        functools.partial(_ffn_kernel, alpha=alpha, ffc=ffc),
        grid=(bsz, s // tm),
        in_specs=[tok(d), tok(ATTN_WIDTH), tok(POOL_WIDTH),
                  pl.BlockSpec((1, N_MOD, d), lambda b, i: (b, 0, 0)),
                  _const_spec(wo.shape), _const_spec(wg.shape), _const_spec(wu.shape),
                  _const_spec(wd.shape), _const_spec(ln.shape)],
        out_specs=tok(d),
        out_shape=jax.ShapeDtypeStruct((bsz, s, d), F32),
        scratch_shapes=[pltpu.VMEM((tm, d), F32)],
        compiler_params=pltpu.CompilerParams(dimension_semantics=("arbitrary", "arbitrary"),
                                             vmem_limit_bytes=VMEM_LIMIT_BYTES),
        name="ffn",
    )(x, attn, pool, mod3, wo, wg, wu, wd, ln)


def _rope_tables(s):
    inv_freq = ROPE_THETA ** (-jnp.arange(0, HEAD_DIM, 2, dtype=F32) / HEAD_DIM)
    ang = jnp.arange(s, dtype=F32)[:, None] * inv_freq[None, :]
    cos, sin = jnp.cos(ang), jnp.sin(ang)
    reps = LANES // HALF_DIM
    cos_l = jnp.tile(cos, (1, reps))
    sin_l = jnp.tile(jnp.concatenate([-sin, sin], axis=1), (1, reps // 2))
    return jnp.stack([cos_l, sin_l]), jnp.stack([cos.T, sin.T])


def _pick_ff_chunk(d_ff):
    for c in (512, 256, 128):
        if d_ff % c == 0:
            return c
    raise ValueError(f"d_ff={d_ff} is not a multiple of {LANES}")


def kernel(x, c, w_mod, b_mod, w_in, w_pool, pool_scale, w_o, ln1_g, ln1_b, w_gate, w_up, w_down,
           ln2_g, ln2_b):
    bsz, s, d = x.shape
    depth = w_mod.shape[0]
    d_ff = w_gate.shape[-1]
    alpha = (2.0 * depth) ** 0.25
    ts = min(512, s)
    tq = min(256, s)
    tk = min(512, s)
    tm = min(512, s)
    ffc = _pick_ff_chunk(d_ff)
    tab, tabt = _rope_tables(s)
    a = ATTN_WIDTH
    for l in range(depth):
        mod3 = _mod_call(c, w_mod[l], b_mod[l]).reshape(bsz, N_MOD, d)
        wi = w_in[l]
        wq = (wi[:, 0:a] * (HEAD_DIM ** -0.5 * LOG2_E)).astype(BF16)
        wk = wi[:, a:2 * a].astype(BF16)
        wv = wi[:, 2 * a:3 * a].astype(BF16)
        wu = wi[:, 3 * a:3 * a + POOL_WIDTH].astype(BF16)
        o = 3 * a + POOL_WIDTH
        wiq = (wi[:, o:o + IDX_HEADS * IDX_DIM] * IDX_DIM ** -0.5).astype(BF16)
        o += IDX_HEADS * IDX_DIM
        wik = wi[:, o:o + IDX_DIM]
        wiw = wi[:, o + IDX_DIM:o + IDX_DIM + IDX_HEADS]
        wix = jnp.concatenate([wik, wik, wiw, jnp.zeros((d, LANES - IDX_HEADS), F32)], axis=1).astype(BF16)
        qt, k, vt, iqt, ik, wt, pool = _proj_call(
            x, mod3, tab, tabt, wq, wk, wv, wu, wiq, wix, w_pool[l].astype(BF16),
            pool_scale[l].reshape(1, POOL_WIDTH), ts=ts, tk=tk)
        attn = _attn_call(qt, k, vt, iqt, ik, wt, tq=tq, tk=tk)
        ln = jnp.stack([ln1_g[l], ln1_b[l], ln2_g[l], ln2_b[l]])
        x = _ffn_call(x, attn, pool, mod3, w_o[l].astype(BF16), w_gate[l].astype(BF16),
                      w_up[l].astype(BF16), w_down[l].astype(BF16), ln, tm=tm, alpha=alpha, ffc=ffc)
    return x
```

```python
import functools

import numpy as np
import jax
import jax.numpy as jnp
from jax import lax
from jax.experimental import pallas as pl
from jax.experimental.pallas import tpu as pltpu

N_HEADS = 8
HEAD_DIM = 64
HALF_DIM = HEAD_DIM // 2
ATTN_WIDTH = N_HEADS * HEAD_DIM
IDX_HEADS = 8
IDX_DIM = 64
TOPK_MAX = 256
POOL_WINDOWS = (2, 4, 8, 16)
N_POOL_GROUPS = len(POOL_WINDOWS)
POOL_GROUP_DIM = 128
POOL_WIDTH = N_POOL_GROUPS * POOL_GROUP_DIM
ROPE_THETA = 10000.0
LN_EPS = 1e-5
N_MOD = 6
LOG2_E = 1.4426950408889634

LANES = 128
SUBLANES = 8
POOL_LEVELS = 4
POOL_HALO = SUBLANES * POOL_LEVELS
VMEM_LIMIT_BYTES = 56 * 1024 * 1024
SCORE_BUFFER_BYTES = 16 * 1024 * 1024

F32 = jnp.float32
BF16 = jnp.bfloat16
INT_MIN = np.int32(-2 ** 31)
KEY_NEG_INF = np.int32(np.int64(0x807FFFFF) - 2 ** 32)


def _const_spec(shape):
    zeros = (0,) * len(shape)
    return pl.BlockSpec(shape, lambda *_: zeros, pipeline_mode=pl.Buffered(1))


def _layer_norm(y, g, b):
    mu = jnp.mean(y, axis=-1, keepdims=True)
    d = y - mu
    var = jnp.mean(d * d, axis=-1, keepdims=True)
    return d * lax.rsqrt(var + LN_EPS) * g + b


def _mod_kernel(c_ref, w_ref, b_ref, o_ref):
    c = c_ref[...]
    a = c * jax.nn.sigmoid(c)
    o_ref[...] = jnp.dot(a, w_ref[...], preferred_element_type=F32,
                         precision=lax.Precision.HIGHEST) + b_ref[...]


def _mod_call(c, w_mod, b_mod):
    bsz, d = c.shape
    n = w_mod.shape[1]
    tn = 1024
    return pl.pallas_call(
        _mod_kernel,
        grid=(n // tn,),
        in_specs=[pl.BlockSpec((bsz, d), lambda j: (0, 0)),
                  pl.BlockSpec((d, tn), lambda j: (0, j)),
                  pl.BlockSpec((1, tn), lambda j: (0, j))],
        out_specs=pl.BlockSpec((bsz, tn), lambda j: (0, j)),
        out_shape=jax.ShapeDtypeStruct((bsz, n), F32),
        compiler_params=pltpu.CompilerParams(dimension_semantics=("arbitrary",),
                                             vmem_limit_bytes=VMEM_LIMIT_BYTES),
        name="mod",
    )(c, w_mod, b_mod.reshape(1, n))


def _proj_kernel(x_ref, mod_ref, tab_ref, tabt_ref, wq_ref, wk_ref, wv_ref, wu_ref, wiq_ref, wix_ref,
                 wpool_ref, ps_ref,
                 qt_ref, k_ref, vt_ref, iqt_ref, ik_ref, wt_ref, pool_ref, ubuf_ref, lvl_ref, *, ts, tk):
    si = pl.program_id(1)

    @pl.when(si == 0)
    def _():
        ubuf_ref[0:POOL_HALO, :] = jnp.zeros((POOL_HALO, POOL_WIDTH), F32)

    m = mod_ref[0]
    shift1 = m[0:1, :]
    scale1 = m[1:2, :]
    h = (x_ref[0] * (1.0 + scale1) + shift1).astype(BF16)

    cos = tab_ref[0]
    sin = tab_ref[1]
    lane = lax.broadcasted_iota(jnp.int32, (ts, LANES), 1)
    first_half = (lane & HALF_DIM) == 0

    def rope(y):
        outs = []
        for j in range(y.shape[1] // LANES):
            yj = y[:, j * LANES:(j + 1) * LANES]
            partner = jnp.where(first_half, pltpu.roll(yj, LANES - HALF_DIM, 1),
                                pltpu.roll(yj, HALF_DIM, 1))
            outs.append(yj * cos + partner * sin)
        return outs[0] if len(outs) == 1 else jnp.concatenate(outs, axis=1)

    cos_t = tabt_ref[0]
    sin_t = tabt_ref[1]

    def rope_t(yt):
        outs = []
        for hd in range(yt.shape[0] // HEAD_DIM):
            x1 = yt[hd * HEAD_DIM:hd * HEAD_DIM + HALF_DIM, :]
            x2 = yt[hd * HEAD_DIM + HALF_DIM:(hd + 1) * HEAD_DIM, :]
            outs.append(x1 * cos_t - x2 * sin_t)
            outs.append(x2 * cos_t + x1 * sin_t)
        return jnp.concatenate(outs, axis=0)

    def proj(w_ref):
        return jnp.dot(h, w_ref[...], preferred_element_type=F32)

    qt_ref[0] = rope_t(proj(wq_ref).T).astype(BF16)
    k_ref[0] = rope(proj(wk_ref)).astype(BF16)
    vt = proj(wv_ref).T.astype(BF16)
    for cc in range(ts // tk):
        vt_ref[0, cc] = vt[:, cc * tk:(cc + 1) * tk]
    iqt_ref[0] = rope_t(proj(wiq_ref).T).astype(BF16)
    ix = proj(wix_ref)
    ik_ref[0] = rope(ix[:, :LANES]).astype(BF16)
    wt_ref[0] = (ix[:, LANES:].T)[0:IDX_HEADS, :] * (IDX_HEADS ** -0.5)

    u = proj(wu_ref)
    ubuf_ref[POOL_HALO:POOL_HALO + ts, :] = u
    n_rows = ts + POOL_HALO
    levels = [ubuf_ref]
    for k in range(1, POOL_LEVELS):
        start, shift = SUBLANES * k, 1 << (k - 1)
        lane0 = next(g for g, win in enumerate(POOL_WINDOWS) if win >= (1 << k)) * POOL_GROUP_DIM
        below = levels[-1]
        lvl_ref[k - 1, start:n_rows, lane0:] = (below[start:n_rows, lane0:]
                                                + below[start - shift:n_rows - shift, lane0:])
        levels.append(lvl_ref.at[k - 1])
    t_pos = si * ts + lax.broadcasted_iota(jnp.int32, (ts, POOL_GROUP_DIM), 0)
    for g, win in enumerate(POOL_WINDOWS):
        lo, hi = g * POOL_GROUP_DIM, (g + 1) * POOL_GROUP_DIM
        ug = u[:, lo:hi]
        e = win.bit_length() - 1
        if e < POOL_LEVELS:
            acc = levels[e][POOL_HALO:n_rows, lo:hi]
        else:
            top, half = levels[POOL_LEVELS - 1], win // 2
            acc = top[POOL_HALO:n_rows, lo:hi] + top[POOL_HALO - half:n_rows - half, lo:hi]
        cnt = jnp.minimum(t_pos + 1, win).astype(F32)
        pooled = acc / cnt - ug
        po = jnp.dot(pooled.astype(BF16), wpool_ref[g], preferred_element_type=F32) * ps_ref[:, lo:hi]
        pool_ref[0, :, lo:hi] = po.astype(BF16)
    ubuf_ref[0:POOL_HALO, :] = ubuf_ref[ts:ts + POOL_HALO, :]


def _proj_call(x, mod3, tab, tabt, wq, wk, wv, wu, wiq, wix, wpool, ps, *, ts, tk):
    bsz, s, d = x.shape
    tok = lambda w: pl.BlockSpec((1, ts, w), lambda b, i: (b, i, 0))
    feat = lambda r: pl.BlockSpec((1, r, ts), lambda b, i: (b, 0, i))
    cpt = ts // tk
    out_shape = [jax.ShapeDtypeStruct((bsz, ATTN_WIDTH, s), BF16),
                 jax.ShapeDtypeStruct((bsz, s, ATTN_WIDTH), BF16),
                 jax.ShapeDtypeStruct((bsz, s // tk, ATTN_WIDTH, tk), BF16),
                 jax.ShapeDtypeStruct((bsz, ATTN_WIDTH, s), BF16),
                 jax.ShapeDtypeStruct((bsz, s, LANES), BF16),
                 jax.ShapeDtypeStruct((bsz, IDX_HEADS, s), F32),
                 jax.ShapeDtypeStruct((bsz, s, POOL_WIDTH), BF16)]
    out_specs = [feat(ATTN_WIDTH), tok(ATTN_WIDTH),
                 pl.BlockSpec((1, cpt, ATTN_WIDTH, tk), lambda b, i: (b, i, 0, 0)),
                 feat(ATTN_WIDTH), tok(LANES), feat(IDX_HEADS), tok(POOL_WIDTH)]
    return pl.pallas_call(
        functools.partial(_proj_kernel, ts=ts, tk=tk),
        grid=(bsz, s // ts),
        in_specs=[tok(d),
                  pl.BlockSpec((1, N_MOD, d), lambda b, i: (b, 0, 0)),
                  pl.BlockSpec((2, ts, LANES), lambda b, i: (0, i, 0)),
                  pl.BlockSpec((2, HALF_DIM, ts), lambda b, i: (0, 0, i)),
                  _const_spec(wq.shape), _const_spec(wk.shape), _const_spec(wv.shape),
                  _const_spec(wu.shape), _const_spec(wiq.shape), _const_spec(wix.shape),
                  _const_spec(wpool.shape), _const_spec(ps.shape)],
        out_specs=out_specs,
        out_shape=out_shape,
        scratch_shapes=[pltpu.VMEM((ts + POOL_HALO, POOL_WIDTH), F32),
                        pltpu.VMEM((POOL_LEVELS - 1, ts + POOL_HALO, POOL_WIDTH), F32)],
        compiler_params=pltpu.CompilerParams(dimension_semantics=("arbitrary", "arbitrary"),
                                             vmem_limit_bytes=VMEM_LIMIT_BYTES),
        name="proj",
    )(x, mod3, tab, tabt, wq, wk, wv, wu, wiq, wix, wpool, ps)


def _key_to_float(key):
    key = jnp.maximum(key, KEY_NEG_INF)
    bits = key ^ ((key >> 31) & np.int32(0x7FFFFFFF))
    return pltpu.bitcast(bits, F32)


def _ukey16_to_float(u):
    u = jnp.maximum(u, UKEY16_NEG_INF)
    bits16 = jnp.where(u >= 0x8000, u ^ 0x8000, u ^ 0xFFFF)
    return pltpu.bitcast(lax.shift_left(bits16, 16), F32)


FOLD_CHAINS = 2
BF16_ROWS = 16
BF16_EXACT_INT = 256
UKEY16_NEG_INF = 0x007F
COARSE_BELOW = 0x8001
COARSE_ABOVE = 0x10001
FINE_STEPS = (COARSE_BELOW + COARSE_ABOVE).bit_length()


def _fold_rows(x, op, rows=SUBLANES):
    blocks = [x[r:r + rows] for r in range(0, x.shape[0], rows)]
    chains = blocks[:FOLD_CHAINS]
    for i, blk in enumerate(blocks[FOLD_CHAINS:]):
        chains[i % FOLD_CHAINS] = op(chains[i % FOLD_CHAINS], blk)
    while len(chains) > 1:
        chains = [op(chains[i], chains[i + 1]) for i in range(0, len(chains) - 1, 2)] + (
            [chains[-1]] if len(chains) % 2 else [])
    return chains[0]


def _attn_kernel(qt_ref, k_ref, vt_ref, iqt_ref, ik_ref, wt_ref, tril_ref, o_ref,
                 sc_ref, scb_ref, bias_ref, s_ref, acc_ref, qm_ref, iqm_ref, *, tq, tk, k_sel):
    qi = pl.program_id(1)
    n_chunks = lax.div(qi * tq + tq + tk - 1, tk)

    zeros_half = jnp.zeros((HEAD_DIM, tq), BF16)
    for h in range(N_HEADS):
        lo = h * HEAD_DIM
        pad = (lambda own: [own, zeros_half]) if h % 2 == 0 else (lambda own: [zeros_half, own])
        iqm_ref[h] = jnp.concatenate(pad(iqt_ref[0, lo:lo + HEAD_DIM, :]), axis=0)
        qm_ref[h // 2, :, (h % 2) * tq:(h % 2 + 1) * tq] = jnp.concatenate(
            pad(qt_ref[0, lo:lo + HEAD_DIM, :]), axis=0)

    def chunk_pairs(body, init):
        carry = lax.fori_loop(0, lax.shift_right_logical(n_chunks, 1),
                              lambda i, cr: body(2 * i + 1, body(2 * i, cr)), init)
        return lax.cond((n_chunks & 1) == 1, lambda cr: body(n_chunks - 1, cr), lambda cr: cr, carry)

    t_idx = qi * tq + lax.broadcasted_iota(jnp.int32, (tk, tq), 1)
    s_iota = lax.broadcasted_iota(jnp.int32, (tk, tq), 0)

    def idx_body(c, carry):
        off = pl.multiple_of(c * tk, tk)
        ikc = ik_ref[0, pl.ds(off, tk), :]
        acc = jnp.zeros((tk, tq), F32)
        for h in range(IDX_HEADS):
            lg = jnp.dot(ikc, iqm_ref[h], preferred_element_type=F32)
            acc = acc + wt_ref[0, h:h + 1, :] * jnp.maximum(lg, 0.0)
        sc = jnp.where(s_iota + off <= t_idx, acc, -jnp.inf)
        sc_ref[c] = sc
        scb_ref[c] = sc.astype(BF16)
        return carry

    chunk_pairs(idx_body, 0)

    def count(pred):
        def body(c, cnt):
            return cnt + _fold_rows(jnp.where(pred(sc_ref[c], c), 1.0, 0.0), jnp.add)
        cnt = lax.fori_loop(0, n_chunks, body, jnp.zeros((SUBLANES, tq), F32))
        return jnp.sum(cnt, axis=0, keepdims=True)

    def count_coarse(trial_b):
        one, zero = jnp.ones((), BF16), jnp.zeros((), BF16)

        def body(c, cnt):
            return cnt + _fold_rows(jnp.where(scb_ref[c] >= trial_b, one, zero), jnp.add, BF16_ROWS)
        cnt = lax.fori_loop(0, n_chunks, body, jnp.zeros((BF16_ROWS, tq), BF16))
        return jnp.sum(cnt.astype(F32), axis=0, keepdims=True)

    def coarse_body(i, cand):
        trial = cand | lax.shift_left(jnp.int32(1), 15 - i)
        cnt = count_coarse(_ukey16_to_float(trial).astype(BF16))
        return jnp.where(cnt >= k_sel, trial, cand)

    cand16 = lax.fori_loop(0, 16, coarse_body, jnp.zeros((1, tq), jnp.int32))
    cand16 = jnp.maximum(cand16, UKEY16_NEG_INF)
    center = (lax.shift_left(cand16, 16) | jnp.where(cand16 >= 0x8000, 0, 0xFFFF)) ^ INT_MIN

    def fine_body(i, carry):
        lo, hi, cnt_hi = carry
        mid = lo + lax.shift_right_logical(hi - lo, 1)
        cnt = count(lambda sc, c: sc >= _key_to_float(mid))
        feasible = cnt >= k_sel
        return jnp.where(feasible, mid, lo), jnp.where(feasible, hi, mid), jnp.where(feasible, cnt_hi, cnt)

    lo, _, cnt_hi = lax.fori_loop(
        0, FINE_STEPS, fine_body,
        (jnp.maximum(center - COARSE_BELOW, KEY_NEG_INF), center + COARSE_ABOVE, jnp.full((1, tq), -1.0, F32)))
    thr = _key_to_float(lo)

    cnt_gt = lax.cond(jnp.min(cnt_hi) < 0.0, lambda: count(lambda sc, c: sc > thr), lambda: cnt_hi)
    need = jnp.where(thr > -jnp.inf, k_sel - cnt_gt, 0.0)

    def bias_body(c, before):
        sc = sc_ref[c]
        tied = sc == thr
        rank = before + jnp.dot(tril_ref[...], jnp.where(tied, 1.0, 0.0).astype(BF16),
                                preferred_element_type=F32)
        keep = jnp.where(sc > thr, 0.0, jnp.where(tied, jnp.where(rank <= need, 0.0, -jnp.inf), -jnp.inf))
        bias_ref[c] = keep
        return rank[tk - 1:tk, :]

    chunk_pairs(bias_body, jnp.zeros((1, tq), F32))

    n_pairs = N_HEADS // 2

    sweep = s_ref.shape[0]
    acc_ref[...] = jnp.zeros_like(acc_ref)
    out_rows = []
    for j0 in range(0, n_pairs, sweep):
        def score_body(c, mx, j0=j0):
            off = pl.multiple_of(c * tk, tk)
            bias = bias_ref[c]
            bias2 = jnp.concatenate([bias, bias], axis=1)
            new = []
            for jj in range(sweep):
                j = j0 + jj
                kc = k_ref[0, pl.ds(off, tk), j * LANES:(j + 1) * LANES]
                s = jnp.dot(kc, qm_ref[j], preferred_element_type=F32) + bias2
                s_ref[jj, c] = s
                new.append(jnp.maximum(mx[jj], _fold_rows(s, jnp.maximum)))
            return tuple(new)

        mx = chunk_pairs(score_body, (jnp.full((SUBLANES, 2 * tq), -jnp.inf, F32),) * sweep)
        m = [jnp.max(mx[jj], axis=0, keepdims=True) for jj in range(sweep)]

        def pv_body(c, l, j0=j0, m=m):
            new = []
            for jj in range(sweep):
                j = j0 + jj
                p = jnp.exp2(s_ref[jj, c] - m[jj])
                vc = vt_ref[0, c, j * LANES:(j + 1) * LANES, :]
                acc_ref[j] += jnp.dot(vc, p.astype(BF16), preferred_element_type=F32)
                new.append(l[jj] + _fold_rows(p, jnp.add))
            return tuple(new)

        l = chunk_pairs(pv_body, (jnp.zeros((SUBLANES, 2 * tq), F32),) * sweep)
        for jj in range(sweep):
            o = acc_ref[j0 + jj] / jnp.sum(l[jj], axis=0, keepdims=True)
            out_rows.append(o[0:HEAD_DIM, 0:tq])
            out_rows.append(o[HEAD_DIM:LANES, tq:2 * tq])
    o_ref[0] = jnp.concatenate(out_rows, axis=0).T.astype(BF16)


def _attn_call(qt, k, vt, iqt, ik, wt, *, tq, tk):
    bsz, s, _ = k.shape
    k_sel = min(TOPK_MAX, s // 4)
    nc = s // tk
    assert vt.shape == (bsz, nc, ATTN_WIDTH, tk)
    assert s // BF16_ROWS <= BF16_EXACT_INT
    qfeat = lambda r: pl.BlockSpec((1, r, tq), lambda b, i: (b, 0, i))
    full = lambda w: pl.BlockSpec((1, s, w), lambda b, i: (b, 0, 0))
    tril = jnp.tril(jnp.ones((tk, tk), BF16))
    n_pairs = N_HEADS // 2
    sweep = max(1, min(n_pairs, SCORE_BUFFER_BYTES // (s * 2 * tq * 4)))
    assert n_pairs % sweep == 0
    return pl.pallas_call(
        functools.partial(_attn_kernel, tq=tq, tk=tk, k_sel=k_sel),
        grid=(bsz, s // tq),
        in_specs=[qfeat(ATTN_WIDTH), full(ATTN_WIDTH),
                  pl.BlockSpec((1, nc, ATTN_WIDTH, tk), lambda b, i: (b, 0, 0, 0)),
                  qfeat(ATTN_WIDTH), full(LANES), qfeat(IDX_HEADS), _const_spec((tk, tk))],
        out_specs=pl.BlockSpec((1, tq, ATTN_WIDTH), lambda b, i: (b, i, 0)),
        out_shape=jax.ShapeDtypeStruct((bsz, s, ATTN_WIDTH), BF16),
        scratch_shapes=[pltpu.VMEM((nc, tk, tq), F32),
                        pltpu.VMEM((nc, tk, tq), BF16),
                        pltpu.VMEM((nc, tk, tq), F32),
                        pltpu.VMEM((sweep, nc, tk, 2 * tq), F32),
                        pltpu.VMEM((N_HEADS // 2, LANES, 2 * tq), F32),
                        pltpu.VMEM((N_HEADS // 2, LANES, 2 * tq), BF16),
                        pltpu.VMEM((IDX_HEADS, LANES, tq), BF16)],
        compiler_params=pltpu.CompilerParams(dimension_semantics=("arbitrary", "arbitrary"),
                                             vmem_limit_bytes=VMEM_LIMIT_BYTES),
        name="attn",
    )(qt, k, vt, iqt, ik, wt, tril)


def _ffn_kernel(x_ref, attn_ref, pool_ref, mod_ref, wo_ref, wg_ref, wu_ref, wd_ref, ln_ref,
                o_ref, acc_ref, *, alpha, ffc):
    m = mod_ref[0]
    gate1, shift2, scale2, gate2 = m[2:3, :], m[3:4, :], m[4:5, :], m[5:6, :]
    ln = ln_ref[...]
    mix = jnp.dot(jnp.concatenate([attn_ref[0], pool_ref[0]], axis=1), wo_ref[...],
                  preferred_element_type=F32)
    x1 = _layer_norm(alpha * x_ref[0] + gate1 * mix, ln[0:1, :], ln[1:2, :])
    h2 = (x1 * (1.0 + scale2) + shift2).astype(BF16)
    acc_ref[...] = jnp.zeros_like(acc_ref)

    for c0 in range(0, wg_ref.shape[1], ffc):
        g = jnp.dot(h2, wg_ref[:, c0:c0 + ffc], preferred_element_type=F32)
        u = jnp.dot(h2, wu_ref[:, c0:c0 + ffc], preferred_element_type=F32)
        a = (g * jax.nn.sigmoid(g) * u).astype(BF16)
        acc_ref[...] += jnp.dot(a, wd_ref[c0:c0 + ffc, :], preferred_element_type=F32)
    o_ref[0] = _layer_norm(alpha * x1 + gate2 * acc_ref[...], ln[2:3, :], ln[3:4, :])


def _ffn_call(x, attn, pool, mod3, wo, wg, wu, wd, ln, *, tm, alpha, ffc):
    bsz, s, d = x.shape
    tok = lambda w: pl.BlockSpec((1, tm, w), lambda b, i: (b, i, 0))
    return pl.pallas_call(
        functools.partial(_ffn_kernel, alpha=alpha, ffc=ffc),
        grid=(bsz, s // tm),
        in_specs=[tok(d), tok(ATTN_WIDTH), tok(POOL_WIDTH),
                  pl.BlockSpec((1, N_MOD, d), lambda b, i: (b, 0, 0)),
                  _const_spec(wo.shape), _const_spec(wg.shape), _const_spec(wu.shape),
                  _const_spec(wd.shape), _const_spec(ln.shape)],
        out_specs=tok(d),
        out_shape=jax.ShapeDtypeStruct((bsz, s, d), F32),
        scratch_shapes=[pltpu.VMEM((tm, d), F32)],
        compiler_params=pltpu.CompilerParams(dimension_semantics=("arbitrary", "arbitrary"),
                                             vmem_limit_bytes=VMEM_LIMIT_BYTES),
        name="ffn",
    )(x, attn, pool, mod3, wo, wg, wu, wd, ln)


def _rope_tables(s):
    inv_freq = ROPE_THETA ** (-jnp.arange(0, HEAD_DIM, 2, dtype=F32) / HEAD_DIM)
    ang = jnp.arange(s, dtype=F32)[:, None] * inv_freq[None, :]
    cos, sin = jnp.cos(ang), jnp.sin(ang)
    reps = LANES // HALF_DIM
    cos_l = jnp.tile(cos, (1, reps))
    sin_l = jnp.tile(jnp.concatenate([-sin, sin], axis=1), (1, reps // 2))
    return jnp.stack([cos_l, sin_l]), jnp.stack([cos.T, sin.T])


def _pick_ff_chunk(d_ff):
    for c in (512, 256, 128):
        if d_ff % c == 0:
            return c
    raise ValueError(f"d_ff={d_ff} is not a multiple of {LANES}")


def kernel(x, c, w_mod, b_mod, w_in, w_pool, pool_scale, w_o, ln1_g, ln1_b, w_gate, w_up, w_down,
           ln2_g, ln2_b):
    bsz, s, d = x.shape
    depth = w_mod.shape[0]
    d_ff = w_gate.shape[-1]
    alpha = (2.0 * depth) ** 0.25
    ts = min(1024, s)
    tq = min(512, s)
    tk = min(512, s)
    tm = min(512, s)
    ffc = _pick_ff_chunk(d_ff)
    tab, tabt = _rope_tables(s)
    a = ATTN_WIDTH
    for l in range(depth):
        mod3 = _mod_call(c, w_mod[l], b_mod[l]).reshape(bsz, N_MOD, d)
        wi = w_in[l]
        wq = (wi[:, 0:a] * (HEAD_DIM ** -0.5 * LOG2_E)).astype(BF16)
        wk = wi[:, a:2 * a].astype(BF16)
        wv = wi[:, 2 * a:3 * a].astype(BF16)
        wu = wi[:, 3 * a:3 * a + POOL_WIDTH].astype(BF16)
        o = 3 * a + POOL_WIDTH
        wiq = (wi[:, o:o + IDX_HEADS * IDX_DIM] * IDX_DIM ** -0.5).astype(BF16)
        o += IDX_HEADS * IDX_DIM
        wik = wi[:, o:o + IDX_DIM]
        wiw = wi[:, o + IDX_DIM:o + IDX_DIM + IDX_HEADS]
        wix = jnp.concatenate([wik, wik, wiw, jnp.zeros((d, LANES - IDX_HEADS), F32)], axis=1).astype(BF16)
        qt, k, vt, iqt, ik, wt, pool = _proj_call(
            x, mod3, tab, tabt, wq, wk, wv, wu, wiq, wix, w_pool[l].astype(BF16),
            pool_scale[l].reshape(1, POOL_WIDTH), ts=ts, tk=tk)
        attn = _attn_call(qt, k, vt, iqt, ik, wt, tq=tq, tk=tk)
        ln = jnp.stack([ln1_g[l], ln1_b[l], ln2_g[l], ln2_b[l]])
        x = _ffn_call(x, attn, pool, mod3, w_o[l].astype(BF16), w_gate[l].astype(BF16),
                      w_up[l].astype(BF16), w_down[l].astype(BF16), ln, tm=tm, alpha=alpha, ffc=ffc)
    return x
```

```python
import functools

import numpy as np
import jax
import jax.numpy as jnp
from jax import lax
from jax.experimental import pallas as pl
from jax.experimental.pallas import tpu as pltpu

N_HEADS = 8
HEAD_DIM = 64
HALF_DIM = HEAD_DIM // 2
ATTN_WIDTH = N_HEADS * HEAD_DIM
IDX_HEADS = 8
IDX_DIM = 64
TOPK_MAX = 256
POOL_WINDOWS = (2, 4, 8, 16)
N_POOL_GROUPS = len(POOL_WINDOWS)
POOL_GROUP_DIM = 128
POOL_WIDTH = N_POOL_GROUPS * POOL_GROUP_DIM
ROPE_THETA = 10000.0
LN_EPS = 1e-5
N_MOD = 6
LOG2_E = 1.4426950408889634

LANES = 128
SUBLANES = 8
POOL_LEVELS = 4
POOL_HALO = SUBLANES * POOL_LEVELS
VMEM_LIMIT_BYTES = 56 * 1024 * 1024
SCORE_BUFFER_BYTES = 16 * 1024 * 1024

F32 = jnp.float32
BF16 = jnp.bfloat16
INT_MIN = np.int32(-2 ** 31)
KEY_NEG_INF = np.int32(np.int64(0x807FFFFF) - 2 ** 32)


def _const_spec(shape):
    zeros = (0,) * len(shape)
    return pl.BlockSpec(shape, lambda *_: zeros, pipeline_mode=pl.Buffered(1))


def _layer_norm(y, g, b):
    mu = jnp.mean(y, axis=-1, keepdims=True)
    d = y - mu
    var = jnp.mean(d * d, axis=-1, keepdims=True)
    return d * lax.rsqrt(var + LN_EPS) * g + b


def _mod_kernel(c_ref, w_ref, b_ref, o_ref):
    c = c_ref[...]
    a = c * jax.nn.sigmoid(c)
    o_ref[...] = jnp.dot(a, w_ref[...], preferred_element_type=F32,
                         precision=lax.Precision.HIGHEST) + b_ref[...]


def _mod_call(c, w_mod, b_mod):
    bsz, d = c.shape
    n = w_mod.shape[1]
    tn = 1024
    return pl.pallas_call(
        _mod_kernel,
        grid=(n // tn,),
        in_specs=[pl.BlockSpec((bsz, d), lambda j: (0, 0)),
                  pl.BlockSpec((d, tn), lambda j: (0, j)),
                  pl.BlockSpec((1, tn), lambda j: (0, j))],
        out_specs=pl.BlockSpec((bsz, tn), lambda j: (0, j)),
        out_shape=jax.ShapeDtypeStruct((bsz, n), F32),
        compiler_params=pltpu.CompilerParams(dimension_semantics=("arbitrary",),
                                             vmem_limit_bytes=VMEM_LIMIT_BYTES),
        name="mod",
    )(c, w_mod, b_mod.reshape(1, n))


def _proj_kernel(x_ref, mod_ref, tab_ref, tabt_ref, wq_ref, wk_ref, wv_ref, wu_ref, wiq_ref, wix_ref,
                 wpool_ref, ps_ref,
                 qt_ref, k_ref, vt_ref, iqt_ref, ik_ref, wt_ref, pool_ref, ubuf_ref, lvl_ref, *, ts, tk):
    si = pl.program_id(1)

    @pl.when(si == 0)
    def _():
        ubuf_ref[0:POOL_HALO, :] = jnp.zeros((POOL_HALO, POOL_WIDTH), F32)

    m = mod_ref[0]
    shift1 = m[0:1, :]
    scale1 = m[1:2, :]
    h = (x_ref[0] * (1.0 + scale1) + shift1).astype(BF16)

    cos = tab_ref[0]
    sin = tab_ref[1]
    lane = lax.broadcasted_iota(jnp.int32, (ts, LANES), 1)
    first_half = (lane & HALF_DIM) == 0

    def rope(y):
        outs = []
        for j in range(y.shape[1] // LANES):
            yj = y[:, j * LANES:(j + 1) * LANES]
            partner = jnp.where(first_half, pltpu.roll(yj, LANES - HALF_DIM, 1),
                                pltpu.roll(yj, HALF_DIM, 1))
            outs.append(yj * cos + partner * sin)
        return outs[0] if len(outs) == 1 else jnp.concatenate(outs, axis=1)

    cos_t = tabt_ref[0]
    sin_t = tabt_ref[1]

    def rope_t(yt):
        outs = []
        for hd in range(yt.shape[0] // HEAD_DIM):
            x1 = yt[hd * HEAD_DIM:hd * HEAD_DIM + HALF_DIM, :]
            x2 = yt[hd * HEAD_DIM + HALF_DIM:(hd + 1) * HEAD_DIM, :]
            outs.append(x1 * cos_t - x2 * sin_t)
            outs.append(x2 * cos_t + x1 * sin_t)
        return jnp.concatenate(outs, axis=0)

    def proj(w_ref):
        return jnp.dot(h, w_ref[...], preferred_element_type=F32)

    qt_ref[0] = rope_t(proj(wq_ref).T).astype(BF16)
    k_ref[0] = rope(proj(wk_ref)).astype(BF16)
    vt = proj(wv_ref).T.astype(BF16)
    for cc in range(ts // tk):
        vt_ref[0, cc] = vt[:, cc * tk:(cc + 1) * tk]
    iqt_ref[0] = rope_t(proj(wiq_ref).T).astype(BF16)
    ix = proj(wix_ref)
    ik_ref[0] = rope(ix[:, :LANES]).astype(BF16)
    wt_ref[0] = (ix[:, LANES:].T)[0:IDX_HEADS, :] * (IDX_HEADS ** -0.5)

    u = proj(wu_ref)
    ubuf_ref[POOL_HALO:POOL_HALO + ts, :] = u
    n_rows = ts + POOL_HALO
    levels = [ubuf_ref]
    for k in range(1, POOL_LEVELS):
        start, shift = SUBLANES * k, 1 << (k - 1)
        lane0 = next(g for g, win in enumerate(POOL_WINDOWS) if win >= (1 << k)) * POOL_GROUP_DIM
        below = levels[-1]
        lvl_ref[k - 1, start:n_rows, lane0:] = (below[start:n_rows, lane0:]
                                                + below[start - shift:n_rows - shift, lane0:])
        levels.append(lvl_ref.at[k - 1])
    t_pos = si * ts + lax.broadcasted_iota(jnp.int32, (ts, POOL_GROUP_DIM), 0)
    for g, win in enumerate(POOL_WINDOWS):
        lo, hi = g * POOL_GROUP_DIM, (g + 1) * POOL_GROUP_DIM
        ug = u[:, lo:hi]
        e = win.bit_length() - 1
        if e < POOL_LEVELS:
            acc = levels[e][POOL_HALO:n_rows, lo:hi]
        else:
            top, half = levels[POOL_LEVELS - 1], win // 2
            acc = top[POOL_HALO:n_rows, lo:hi] + top[POOL_HALO - half:n_rows - half, lo:hi]
        cnt = jnp.minimum(t_pos + 1, win).astype(F32)
        pooled = acc / cnt - ug
        po = jnp.dot(pooled.astype(BF16), wpool_ref[g], preferred_element_type=F32) * ps_ref[:, lo:hi]
        pool_ref[0, :, lo:hi] = po.astype(BF16)
    ubuf_ref[0:POOL_HALO, :] = ubuf_ref[ts:ts + POOL_HALO, :]


def _proj_call(x, mod3, tab, tabt, wq, wk, wv, wu, wiq, wix, wpool, ps, *, ts, tk):
    bsz, s, d = x.shape
    tok = lambda w: pl.BlockSpec((1, ts, w), lambda b, i: (b, i, 0))
    feat = lambda r: pl.BlockSpec((1, r, ts), lambda b, i: (b, 0, i))
    cpt = ts // tk
    out_shape = [jax.ShapeDtypeStruct((bsz, ATTN_WIDTH, s), BF16),
                 jax.ShapeDtypeStruct((bsz, s, ATTN_WIDTH), BF16),
                 jax.ShapeDtypeStruct((bsz, s // tk, ATTN_WIDTH, tk), BF16),
                 jax.ShapeDtypeStruct((bsz, ATTN_WIDTH, s), BF16),
                 jax.ShapeDtypeStruct((bsz, s, LANES), BF16),
                 jax.ShapeDtypeStruct((bsz, IDX_HEADS, s), F32),
                 jax.ShapeDtypeStruct((bsz, s, POOL_WIDTH), BF16)]
    out_specs = [feat(ATTN_WIDTH), tok(ATTN_WIDTH),
                 pl.BlockSpec((1, cpt, ATTN_WIDTH, tk), lambda b, i: (b, i, 0, 0)),
                 feat(ATTN_WIDTH), tok(LANES), feat(IDX_HEADS), tok(POOL_WIDTH)]
    return pl.pallas_call(
        functools.partial(_proj_kernel, ts=ts, tk=tk),
        grid=(bsz, s // ts),
        in_specs=[tok(d),
                  pl.BlockSpec((1, N_MOD, d), lambda b, i: (b, 0, 0)),
                  pl.BlockSpec((2, ts, LANES), lambda b, i: (0, i, 0)),
                  pl.BlockSpec((2, HALF_DIM, ts), lambda b, i: (0, 0, i)),
                  _const_spec(wq.shape), _const_spec(wk.shape), _const_spec(wv.shape),
                  _const_spec(wu.shape), _const_spec(wiq.shape), _const_spec(wix.shape),
                  _const_spec(wpool.shape), _const_spec(ps.shape)],
        out_specs=out_specs,
        out_shape=out_shape,
        scratch_shapes=[pltpu.VMEM((ts + POOL_HALO, POOL_WIDTH), F32),
                        pltpu.VMEM((POOL_LEVELS - 1, ts + POOL_HALO, POOL_WIDTH), F32)],
        compiler_params=pltpu.CompilerParams(dimension_semantics=("arbitrary", "arbitrary"),
                                             vmem_limit_bytes=VMEM_LIMIT_BYTES),
        name="proj",
    )(x, mod3, tab, tabt, wq, wk, wv, wu, wiq, wix, wpool, ps)


def _key_to_float(key):
    key = jnp.maximum(key, KEY_NEG_INF)
    bits = key ^ ((key >> 31) & np.int32(0x7FFFFFFF))
    return pltpu.bitcast(bits, F32)


def _ukey16_to_float(u):
    u = jnp.maximum(u, UKEY16_NEG_INF)
    bits16 = jnp.where(u >= 0x8000, u ^ 0x8000, u ^ 0xFFFF)
    return pltpu.bitcast(lax.shift_left(bits16, 16), F32)


FOLD_CHAINS = 1
BF16_ROWS = 16
BF16_EXACT_INT = 256
UKEY16_NEG_INF = 0x007F
COARSE_BELOW = 0x8001
COARSE_ABOVE = 0x10001
FINE_STEPS = (COARSE_BELOW + COARSE_ABOVE).bit_length()


def _fold_rows(x, op, rows=SUBLANES):
    blocks = [x[r:r + rows] for r in range(0, x.shape[0], rows)]
    chains = blocks[:FOLD_CHAINS]
    for i, blk in enumerate(blocks[FOLD_CHAINS:]):
        chains[i % FOLD_CHAINS] = op(chains[i % FOLD_CHAINS], blk)
    while len(chains) > 1:
        chains = [op(chains[i], chains[i + 1]) for i in range(0, len(chains) - 1, 2)] + (
            [chains[-1]] if len(chains) % 2 else [])
    return chains[0]


def _attn_kernel(qt_ref, k_ref, vt_ref, iqt_ref, ik_ref, wt_ref, tril_ref, o_ref,
                 sc_ref, scb_ref, bias_ref, s_ref, acc_ref, qm_ref, iqm_ref, *, tq, tk, k_sel):
    qi = pl.program_id(1)
    n_chunks = lax.div(qi * tq + tq + tk - 1, tk)

    zeros_half = jnp.zeros((HEAD_DIM, tq), BF16)
    for h in range(N_HEADS):
        lo = h * HEAD_DIM
        pad = (lambda own: [own, zeros_half]) if h % 2 == 0 else (lambda own: [zeros_half, own])
        iqm_ref[h] = jnp.concatenate(pad(iqt_ref[0, lo:lo + HEAD_DIM, :]), axis=0)
        qm_ref[h // 2, :, (h % 2) * tq:(h % 2 + 1) * tq] = jnp.concatenate(
            pad(qt_ref[0, lo:lo + HEAD_DIM, :]), axis=0)

    def chunk_pairs(body, init):
        carry = lax.fori_loop(0, lax.shift_right_logical(n_chunks, 1),
                              lambda i, cr: body(2 * i + 1, body(2 * i, cr)), init)
        return lax.cond((n_chunks & 1) == 1, lambda cr: body(n_chunks - 1, cr), lambda cr: cr, carry)

    t_idx = qi * tq + lax.broadcasted_iota(jnp.int32, (tk, tq), 1)
    s_iota = lax.broadcasted_iota(jnp.int32, (tk, tq), 0)

    def idx_body(c, carry):
        off = pl.multiple_of(c * tk, tk)
        ikc = ik_ref[0, pl.ds(off, tk), :]
        acc = jnp.zeros((tk, tq), F32)
        for h in range(IDX_HEADS):
            lg = jnp.dot(ikc, iqm_ref[h], preferred_element_type=F32)
            acc = acc + wt_ref[0, h:h + 1, :] * jnp.maximum(lg, 0.0)
        sc = jnp.where(s_iota + off <= t_idx, acc, -jnp.inf)
        sc_ref[c] = sc
        scb_ref[c] = sc.astype(BF16)
        return carry

    chunk_pairs(idx_body, 0)

    def count(pred):
        def body(c, cnt):
            return cnt + _fold_rows(jnp.where(pred(sc_ref[c], c), 1.0, 0.0), jnp.add)
        cnt = lax.fori_loop(0, n_chunks, body, jnp.zeros((SUBLANES, tq), F32))
        return jnp.sum(cnt, axis=0, keepdims=True)

    def count_coarse(trial_b):
        one, zero = jnp.ones((), BF16), jnp.zeros((), BF16)

        def body(c, cnt):
            return cnt + _fold_rows(jnp.where(scb_ref[c] >= trial_b, one, zero), jnp.add, BF16_ROWS)
        cnt = lax.fori_loop(0, n_chunks, body, jnp.zeros((BF16_ROWS, tq), BF16))
        return jnp.sum(cnt.astype(F32), axis=0, keepdims=True)

    def coarse_body(i, cand):
        trial = cand | lax.shift_left(jnp.int32(1), 15 - i)
        cnt = count_coarse(_ukey16_to_float(trial).astype(BF16))
        return jnp.where(cnt >= k_sel, trial, cand)

    cand16 = lax.fori_loop(0, 16, coarse_body, jnp.zeros((1, tq), jnp.int32))
    cand16 = jnp.maximum(cand16, UKEY16_NEG_INF)
    center = (lax.shift_left(cand16, 16) | jnp.where(cand16 >= 0x8000, 0, 0xFFFF)) ^ INT_MIN

    def fine_body(i, carry):
        lo, hi, cnt_hi = carry
        mid = lo + lax.shift_right_logical(hi - lo, 1)
        cnt = count(lambda sc, c: sc >= _key_to_float(mid))
        feasible = cnt >= k_sel
        return jnp.where(feasible, mid, lo), jnp.where(feasible, hi, mid), jnp.where(feasible, cnt_hi, cnt)

    lo, _, cnt_hi = lax.fori_loop(
        0, FINE_STEPS, fine_body,
        (jnp.maximum(center - COARSE_BELOW, KEY_NEG_INF), center + COARSE_ABOVE, jnp.full((1, tq), -1.0, F32)))
    thr = _key_to_float(lo)

    cnt_gt = lax.cond(jnp.min(cnt_hi) < 0.0, lambda: count(lambda sc, c: sc > thr), lambda: cnt_hi)
    need = jnp.where(thr > -jnp.inf, k_sel - cnt_gt, 0.0)

    def bias_body(c, before):
        sc = sc_ref[c]
        tied = sc == thr
        rank = before + jnp.dot(tril_ref[...], jnp.where(tied, 1.0, 0.0).astype(BF16),
                                preferred_element_type=F32)
        keep = jnp.where(sc > thr, 0.0, jnp.where(tied, jnp.where(rank <= need, 0.0, -jnp.inf), -jnp.inf))
        bias_ref[c] = keep
        return rank[tk - 1:tk, :]

    chunk_pairs(bias_body, jnp.zeros((1, tq), F32))

    n_pairs = N_HEADS // 2

    sweep = s_ref.shape[0]
    acc_ref[...] = jnp.zeros_like(acc_ref)
    out_rows = []
    for j0 in range(0, n_pairs, sweep):
        def score_body(c, mx, j0=j0):
            off = pl.multiple_of(c * tk, tk)
            bias = bias_ref[c]
            bias2 = jnp.concatenate([bias, bias], axis=1)
            new = []
            for jj in range(sweep):
                j = j0 + jj
                kc = k_ref[0, pl.ds(off, tk), j * LANES:(j + 1) * LANES]
                s = jnp.dot(kc, qm_ref[j], preferred_element_type=F32) + bias2
                s_ref[jj, c] = s
                new.append(jnp.maximum(mx[jj], _fold_rows(s, jnp.maximum)))
            return tuple(new)

        mx = chunk_pairs(score_body, (jnp.full((SUBLANES, 2 * tq), -jnp.inf, F32),) * sweep)
        m = [jnp.max(mx[jj], axis=0, keepdims=True) for jj in range(sweep)]

        def pv_body(c, l, j0=j0, m=m):
            new = []
            for jj in range(sweep):
                j = j0 + jj
                p = jnp.exp2(s_ref[jj, c] - m[jj])
                vc = vt_ref[0, c, j * LANES:(j + 1) * LANES, :]
                acc_ref[j] += jnp.dot(vc, p.astype(BF16), preferred_element_type=F32)
                new.append(l[jj] + _fold_rows(p, jnp.add))
            return tuple(new)

        l = chunk_pairs(pv_body, (jnp.zeros((SUBLANES, 2 * tq), F32),) * sweep)
        for jj in range(sweep):
            o = acc_ref[j0 + jj] / jnp.sum(l[jj], axis=0, keepdims=True)
            out_rows.append(o[0:HEAD_DIM, 0:tq])
            out_rows.append(o[HEAD_DIM:LANES, tq:2 * tq])
    o_ref[0] = jnp.concatenate(out_rows, axis=0).T.astype(BF16)


def _attn_call(qt, k, vt, iqt, ik, wt, *, tq, tk):
    bsz, s, _ = k.shape
    k_sel = min(TOPK_MAX, s // 4)
    nc = s // tk
    assert vt.shape == (bsz, nc, ATTN_WIDTH, tk)
    assert s // BF16_ROWS <= BF16_EXACT_INT
    qfeat = lambda r: pl.BlockSpec((1, r, tq), lambda b, i: (b, 0, i))
    full = lambda w: pl.BlockSpec((1, s, w), lambda b, i: (b, 0, 0))
    tril = jnp.tril(jnp.ones((tk, tk), BF16))
    n_pairs = N_HEADS // 2
    sweep = max(1, min(n_pairs, SCORE_BUFFER_BYTES // (s * 2 * tq * 4)))
    assert n_pairs % sweep == 0
    return pl.pallas_call(
        functools.partial(_attn_kernel, tq=tq, tk=tk, k_sel=k_sel),
        grid=(bsz, s // tq),
        in_specs=[qfeat(ATTN_WIDTH), full(ATTN_WIDTH),
                  pl.BlockSpec((1, nc, ATTN_WIDTH, tk), lambda b, i: (b, 0, 0, 0)),
                  qfeat(ATTN_WIDTH), full(LANES), qfeat(IDX_HEADS), _const_spec((tk, tk))],
        out_specs=pl.BlockSpec((1, tq, ATTN_WIDTH), lambda b, i: (b, i, 0)),
        out_shape=jax.ShapeDtypeStruct((bsz, s, ATTN_WIDTH), BF16),
        scratch_shapes=[pltpu.VMEM((nc, tk, tq), F32),
                        pltpu.VMEM((nc, tk, tq), BF16),
                        pltpu.VMEM((nc, tk, tq), F32),
                        pltpu.VMEM((sweep, nc, tk, 2 * tq), F32),
                        pltpu.VMEM((N_HEADS // 2, LANES, 2 * tq), F32),
                        pltpu.VMEM((N_HEADS // 2, LANES, 2 * tq), BF16),
                        pltpu.VMEM((IDX_HEADS, LANES, tq), BF16)],
        compiler_params=pltpu.CompilerParams(dimension_semantics=("arbitrary", "arbitrary"),
                                             vmem_limit_bytes=VMEM_LIMIT_BYTES),
        name="attn",
    )(qt, k, vt, iqt, ik, wt, tril)


def _ffn_kernel(x_ref, attn_ref, pool_ref, mod_ref, wo_ref, wg_ref, wu_ref, wd_ref, ln_ref,
                o_ref, acc_ref, *, alpha, ffc):
    m = mod_ref[0]
    gate1, shift2, scale2, gate2 = m[2:3, :], m[3:4, :], m[4:5, :], m[5:6, :]
    ln = ln_ref[...]
    mix = jnp.dot(jnp.concatenate([attn_ref[0], pool_ref[0]], axis=1), wo_ref[...],
                  preferred_element_type=F32)
    x1 = _layer_norm(alpha * x_ref[0] + gate1 * mix, ln[0:1, :], ln[1:2, :])
    h2 = (x1 * (1.0 + scale2) + shift2).astype(BF16)
    acc_ref[...] = jnp.zeros_like(acc_ref)

    for c0 in range(0, wg_ref.shape[1], ffc):
        g = jnp.dot(h2, wg_ref[:, c0:c0 + ffc], preferred_element_type=F32)
        u = jnp.dot(h2, wu_ref[:, c0:c0 + ffc], preferred_element_type=F32)
        a = (g * jax.nn.sigmoid(g) * u).astype(BF16)
        acc_ref[...] += jnp.dot(a, wd_ref[c0:c0 + ffc, :], preferred_element_type=F32)
    o_ref[0] = _layer_norm(alpha * x1 + gate2 * acc_ref[...], ln[2:3, :], ln[3:4, :])


def _ffn_call(x, attn, pool, mod3, wo, wg, wu, wd, ln, *, tm, alpha, ffc):
    bsz, s, d = x.shape
    tok = lambda w: pl.BlockSpec((1, tm, w), lambda b, i: (b, i, 0))
    return pl.pallas_call(
        functools.partial(_ffn_kernel, alpha=alpha, ffc=ffc),
        grid=(bsz, s // tm),
        in_specs=[tok(d), tok(ATTN_WIDTH), tok(POOL_WIDTH),
                  pl.BlockSpec((1, N_MOD, d), lambda b, i: (b, 0, 0)),
                  _const_spec(wo.shape), _const_spec(wg.shape), _const_spec(wu.shape),
                  _const_spec(wd.shape), _const_spec(ln.shape)],
        out_specs=tok(d),
        out_shape=jax.ShapeDtypeStruct((bsz, s, d), F32),
        scratch_shapes=[pltpu.VMEM((tm, d), F32)],
        compiler_params=pltpu.CompilerParams(dimension_semantics=("arbitrary", "arbitrary"),
                                             vmem_limit_bytes=VMEM_LIMIT_BYTES),
        name="ffn",
    )(x, attn, pool, mod3, wo, wg, wu, wd, ln)


def _rope_tables(s):
    inv_freq = ROPE_THETA ** (-jnp.arange(0, HEAD_DIM, 2, dtype=F32) / HEAD_DIM)
    ang = jnp.arange(s, dtype=F32)[:, None] * inv_freq[None, :]
    cos, sin = jnp.cos(ang), jnp.sin(ang)
    reps = LANES // HALF_DIM
    cos_l = jnp.tile(cos, (1, reps))
    sin_l = jnp.tile(jnp.concatenate([-sin, sin], axis=1), (1, reps // 2))
    return jnp.stack([cos_l, sin_l]), jnp.stack([cos.T, sin.T])


def _pick_ff_chunk(d_ff):
    for c in (512, 256, 128):
        if d_ff % c == 0:
            return c
    raise ValueError(f"d_ff={d_ff} is not a multiple of {LANES}")


def kernel(x, c, w_mod, b_mod, w_in, w_pool, pool_scale, w_o, ln1_g, ln1_b, w_gate, w_up, w_down,
           ln2_g, ln2_b):
    bsz, s, d = x.shape
    depth = w_mod.shape[0]
    d_ff = w_gate.shape[-1]
    alpha = (2.0 * depth) ** 0.25
    ts = min(1024, s)
    tq = min(512, s)
    tk = min(512, s)
    tm = min(512, s)
    ffc = _pick_ff_chunk(d_ff)
    tab, tabt = _rope_tables(s)
    a = ATTN_WIDTH
    for l in range(depth):
        mod3 = _mod_call(c, w_mod[l], b_mod[l]).reshape(bsz, N_MOD, d)
        wi = w_in[l]
        wq = (wi[:, 0:a] * (HEAD_DIM ** -0.5 * LOG2_E)).astype(BF16)
        wk = wi[:, a:2 * a].astype(BF16)
        wv = wi[:, 2 * a:3 * a].astype(BF16)
        wu = wi[:, 3 * a:3 * a + POOL_WIDTH].astype(BF16)
        o = 3 * a + POOL_WIDTH
        wiq = (wi[:, o:o + IDX_HEADS * IDX_DIM] * IDX_DIM ** -0.5).astype(BF16)
        o += IDX_HEADS * IDX_DIM
        wik = wi[:, o:o + IDX_DIM]
        wiw = wi[:, o + IDX_DIM:o + IDX_DIM + IDX_HEADS]
        wix = jnp.concatenate([wik, wik, wiw, jnp.zeros((d, LANES - IDX_HEADS), F32)], axis=1).astype(BF16)
        qt, k, vt, iqt, ik, wt, pool = _proj_call(
            x, mod3, tab, tabt, wq, wk, wv, wu, wiq, wix, w_pool[l].astype(BF16),
            pool_scale[l].reshape(1, POOL_WIDTH), ts=ts, tk=tk)
        attn = _attn_call(qt, k, vt, iqt, ik, wt, tq=tq, tk=tk)
        ln = jnp.stack([ln1_g[l], ln1_b[l], ln2_g[l], ln2_b[l]])
        x = _ffn_call(x, attn, pool, mod3, w_o[l].astype(BF16), w_gate[l].astype(BF16),
                      w_up[l].astype(BF16), w_down[l].astype(BF16), ln, tm=tm, alpha=alpha, ffc=ffc)
    return x
```
